```python
import jax, jax.numpy as jnp
from jax import lax
import numpy as np

D_MODEL = 4096
BATCH = 1
SEQ = 8192
DEPTH = 1
DEC_BATCH = 16
DEC_SEQ = 16
PAST_LEN = 2048

CHUNK = 64
N_PREV_CHUNKS = 8
BAND = (N_PREV_CHUNKS + 1) * CHUNK
RW_HEADS = 32
RW_HEAD_DIM = 64
RW_WIDTH = RW_HEADS * RW_HEAD_DIM
DECAY_LORA = 64
ICL_LORA = 64
GATE_LORA = 256
RW_LN_EPS = 64e-5
RW_COLS = 3 * RW_WIDTH + DECAY_LORA + ICL_LORA + GATE_LORA
ATT_HEADS = 16
ATT_HEAD_DIM = 128
ATT_WIDTH = ATT_HEADS * ATT_HEAD_DIM
REL_CLIP = 256
N_MEM = 256
MEM_HEADS = 4
MEM_HEAD_DIM = 512
MEM_WIDTH = MEM_HEADS * MEM_HEAD_DIM
N_BRANCH = 3
D_IN = RW_COLS + 3 * ATT_WIDTH + MEM_WIDTH + N_BRANCH * D_MODEL
N_GROUPS = 8
EXPERTS_PER_GROUP = 8
N_EXPERTS = N_GROUPS * EXPERTS_PER_GROUP
TOP_K = 2
D_EXPERT = 1024
EXPERT_BLOCK = 128
NORM_EPS = 1e-6
NEG_INF = -1e30

kernel_name = 'hybrid_stream_rwkv7_chunkband_mem_hmoe_step'


def rmsnorm(x, g):
    x32 = x.astype(jnp.float32)
    y = x32 * lax.rsqrt(jnp.mean(x32 * x32, axis=-1, keepdims=True) + NORM_EPS)
    return (y * g.astype(jnp.float32)).astype(x.dtype)


def rwkv7_scan(S0, r, w, k, v, kk, a):
    def step(S, inp):
        r_t, w_t, k_t, v_t, kk_t, a_t = inp
        sa = jnp.einsum('bhij,bhj->bhi', S, -kk_t)
        S = (S * w_t[:, :, None, :] + sa[..., None] * (kk_t * a_t)[:, :, None, :]
             + v_t[..., None] * k_t[:, :, None, :])
        return S, jnp.einsum('bhij,bhj->bhi', S, r_t)
    xs = tuple(jnp.moveaxis(t, 1, 0) for t in (r, w, k, v, kk, a))
    S, ys = lax.scan(step, S0, xs)
    return jnp.moveaxis(ys, 0, 1), S


def rwkv7_mixer(z, shift_prev, S0, lp):
    B, T, _ = z.shape
    f32 = jnp.float32
    z_prev = jnp.concatenate([shift_prev.astype(z.dtype), z[:, :-1]], axis=1)
    zs = z + (z_prev - z) * lp['rwkv_mu']
    cuts = np.cumsum([RW_WIDTH, RW_WIDTH, RW_WIDTH, DECAY_LORA, ICL_LORA]).tolist()
    r, k, v, wd, ad, gd = jnp.split(zs, cuts, axis=-1)
    w_log = -jax.nn.softplus(-(lp['rwkv_w0'] + jnp.tanh(wd) @ lp['rwkv_w2']).astype(f32)) - 0.5
    decay = jnp.exp(-jnp.exp(w_log))
    a = jax.nn.sigmoid((lp['rwkv_a0'] + ad @ lp['rwkv_a2']).astype(f32))
    g = jax.nn.sigmoid(gd) @ lp['rwkv_g2']
    heads = lambda t: t.reshape(B, T, RW_HEADS, RW_HEAD_DIM)
    kk = heads((k * lp['rwkv_k_k']).astype(f32))
    kk = kk / jnp.maximum(jnp.sqrt(jnp.sum(kk * kk, axis=-1, keepdims=True)), 1e-12)
    k_mod = heads(k.astype(f32) * (1.0 + (a - 1.0) * lp['rwkv_k_a'].astype(f32)))
    r_h, v_h = heads(r.astype(f32)), heads(v.astype(f32))
    y, S = rwkv7_scan(S0.astype(f32), r_h, heads(decay), k_mod, v_h, kk, heads(a))
    mean = jnp.mean(y, axis=-1, keepdims=True)
    var = jnp.mean(jnp.square(y - mean), axis=-1, keepdims=True)
    y = ((y - mean) * lax.rsqrt(var + RW_LN_EPS)).reshape(B, T, RW_WIDTH)
    y = y * lp['rwkv_ln_w'].astype(f32) + lp['rwkv_ln_b'].astype(f32)
    bonus = jnp.sum(r_h * k_mod * lp['rwkv_r_k'].astype(f32), axis=-1, keepdims=True) * v_h
    y = y + bonus.reshape(B, T, RW_WIDTH)
    out = (y * g.astype(f32)).astype(z.dtype) @ lp['w_rwkv_out']
    return out, S.astype(S0.dtype), z[:, -1:]


def band_attention(q, k, v, q_pos, k_pos, rel_bias):
    s = jnp.einsum('bcqhd,bckhd->bchqk', q, k).astype(jnp.float32) * (ATT_HEAD_DIM ** -0.5)
    rel = jnp.clip(q_pos[:, :, None] - k_pos[:, None, :], -REL_CLIP, REL_CLIP) + REL_CLIP
    s = s + jnp.moveaxis(rel_bias[:, rel], 0, 1).astype(jnp.float32)
    q_chunk = (q_pos // CHUNK)[:, :, None]
    k_chunk = (k_pos // CHUNK)[:, None, :]
    mask = (k_pos[:, None, :] >= 0) & (k_chunk <= q_chunk) & (k_chunk >= q_chunk - N_PREV_CHUNKS)
    s = jnp.where(mask[None, :, None], s, NEG_INF)
    p = jax.nn.softmax(s, axis=-1)
    return jnp.einsum('bchqk,bckhd->bcqhd', p.astype(v.dtype), v)


def memory_kv(mem, lp):
    B = mem.shape[0]
    kv = rmsnorm(mem, lp['mem_norm_g']) @ lp['w_mem_kv']
    mk = rmsnorm(kv[..., :MEM_WIDTH].reshape(B, N_MEM, MEM_HEADS, MEM_HEAD_DIM), lp['mem_k_g'])
    mv = kv[..., MEM_WIDTH:].reshape(B, N_MEM, MEM_HEADS, MEM_HEAD_DIM)
    return mk, mv


def memory_attention(q, mk, mv):
    s = jnp.einsum('bthd,bmhd->bhtm', q, mk).astype(jnp.float32) * (MEM_HEAD_DIM ** -0.5)
    p = jax.nn.softmax(s, axis=-1)
    return jnp.einsum('bhtm,bmhd->bthd', p.astype(mv.dtype), mv)


def mixing_sublayer(x, shift_prev, S0, att_k_past, att_v_past, mem_k, mem_v, lp):
    B, T, _ = x.shape
    h = rmsnorm(x, lp['norm_mix_g'])
    z = h @ lp['w_in']
    c1 = RW_COLS
    c2 = c1 + 3 * ATT_WIDTH
    c3 = c2 + MEM_WIDTH
    z_rw, z_att, z_mem, z_gate = z[..., :c1], z[..., c1:c2], z[..., c2:c3], z[..., c3:]
    y_rw, S_new, shift_new = rwkv7_mixer(z_rw, shift_prev, S0, lp)
    q, k, v = (z_att[..., i * ATT_WIDTH:(i + 1) * ATT_WIDTH].reshape(B, T, ATT_HEADS, ATT_HEAD_DIM)
               for i in range(3))
    q = rmsnorm(q, lp['att_q_g'])
    k = rmsnorm(k, lp['att_k_g'])
    if att_k_past is None:
        n_chunks = T // CHUNK
        pad = N_PREV_CHUNKS * CHUNK
        idx = jnp.arange(n_chunks)[:, None] * CHUNK + jnp.arange(BAND)[None, :]
        kb = jnp.pad(k, ((0, 0), (pad, 0), (0, 0), (0, 0)))[:, idx]
        vb = jnp.pad(v, ((0, 0), (pad, 0), (0, 0), (0, 0)))[:, idx]
        q_pos = jnp.arange(T).reshape(n_chunks, CHUNK)
        attn = band_attention(q.reshape(B, n_chunks, CHUNK, ATT_HEADS, ATT_HEAD_DIM), kb, vb,
                              q_pos, idx - pad, lp['att_rel_bias'])
        keep = min(pad, T)
        k_rows, v_rows = k[:, T - keep:], v[:, T - keep:]
    else:
        P = att_k_past.shape[1]
        kb = jnp.concatenate([att_k_past.astype(k.dtype), k], axis=1)[:, None]
        vb = jnp.concatenate([att_v_past.astype(v.dtype), v], axis=1)[:, None]
        q_pos = (PAST_LEN + jnp.arange(T))[None]
        k_pos = (PAST_LEN - P + jnp.arange(P + T))[None]
        attn = band_attention(q[:, None], kb, vb, q_pos, k_pos, lp['att_rel_bias'])
        k_rows, v_rows = k, v
    y_att = attn.reshape(B, T, ATT_WIDTH) @ lp['w_att_out']
    q_mem = rmsnorm(z_mem.reshape(B, T, MEM_HEADS, MEM_HEAD_DIM), lp['mem_q_g'])
    y_mem = memory_attention(q_mem, mem_k.astype(x.dtype), mem_v.astype(x.dtype)).reshape(B, T, MEM_WIDTH)
    y_mem = y_mem @ lp['w_mem_out']
    gate = jax.nn.sigmoid(z_gate).reshape(B, T, N_BRANCH, D_MODEL)
    merged = gate[:, :, 0] * y_rw + gate[:, :, 1] * y_att + gate[:, :, 2] * y_mem
    return x + merged @ lp['w_o'], S_new, shift_new, k_rows, v_rows


def grouped_expert_ffn(h, expert_id, weights, w_gate, w_up, w_down):
    N, D = h.shape
    A = N * TOP_K
    n_blocks = A // EXPERT_BLOCK + N_EXPERTS
    rows = n_blocks * EXPERT_BLOCK
    flat_e = expert_id.reshape(A)
    flat_tok = jnp.arange(A, dtype=jnp.int32) // TOP_K
    order = jnp.argsort(flat_e)
    sorted_e = flat_e[order]
    counts = jnp.bincount(flat_e, length=N_EXPERTS)
    start = jnp.cumsum(counts) - counts
    padded = (counts + EXPERT_BLOCK - 1) // EXPERT_BLOCK * EXPERT_BLOCK
    pad_end = jnp.cumsum(padded)
    pad_start = pad_end - padded
    dest = pad_start[sorted_e] + (jnp.arange(A, dtype=jnp.int32) - start[sorted_e])
    buf_tok = jnp.full((rows,), N, jnp.int32).at[dest].set(flat_tok[order])
    buf_w = jnp.zeros((rows,), weights.dtype).at[dest].set(weights.reshape(A)[order])
    block_e = jnp.minimum(jnp.searchsorted(pad_end, jnp.arange(n_blocks) * EXPERT_BLOCK, side='right'),
                          N_EXPERTS - 1)
    h_pad = jnp.concatenate([h, jnp.zeros((1, D), h.dtype)], axis=0)
    xb = h_pad[buf_tok].reshape(n_blocks, EXPERT_BLOCK, D)

    def expert_block(args):
        xe, e = args
        return (jax.nn.silu(xe @ w_gate[e]) * (xe @ w_up[e])) @ w_down[e]

    yb = lax.map(expert_block, (xb, block_e)).reshape(rows, D)
    return jnp.zeros((N + 1, D), h.dtype).at[buf_tok].add(yb * buf_w[:, None])[:N]


def hierarchical_moe(x, lp):
    B, T, D = x.shape
    N = B * T
    h = rmsnorm(x, lp['norm_ffn_g']).reshape(N, D)
    g_prob = jax.nn.softmax((h @ lp['w_router_group']).astype(jnp.float32)
                            + lp['b_router_group'].astype(jnp.float32), axis=-1)
    pg, gidx = lax.top_k(g_prob, 1)
    e_logits = ((h @ lp['w_router_expert']).astype(jnp.float32)
                + lp['b_router_expert'].astype(jnp.float32)).reshape(N, N_GROUPS, EXPERTS_PER_GROUP)
    e_sel = e_logits[jnp.arange(N), gidx[:, 0]]
    pe, eidx = lax.top_k(jax.nn.softmax(e_sel, axis=-1), TOP_K)
    weights = pg * pe / jnp.sum(pe, axis=-1, keepdims=True)
    expert_id = gidx * EXPERTS_PER_GROUP + eidx
    y = grouped_expert_ffn(h, expert_id, weights.astype(x.dtype),
                           lp['w_exp_gate'], lp['w_exp_up'], lp['w_exp_down'])
    return y.reshape(B, T, D)


def setup_inputs(seed: int = 0) -> dict:
    key = jax.random.key(seed)
    keys = jax.random.split(key, 48)
    f32 = jnp.float32
    L = DEPTH
    att_past = min(N_PREV_CHUNKS * CHUNK, PAST_LEN)

    def nrm(i, shape, scale):
        return jax.random.normal(keys[i], shape, f32) * scale

    def uni(i, shape, lo, hi):
        return jax.random.uniform(keys[i], shape, f32, lo, hi)

    return {
        'x_prompt': nrm(0, (BATCH, SEQ, D_MODEL), 1.0),
        'x_sample': nrm(1, (DEC_BATCH, DEC_SEQ, D_MODEL), 1.0),
        'mem_prompt': nrm(2, (BATCH, N_MEM, D_MODEL), 1.0),
        'cache_att_k': nrm(3, (L, DEC_BATCH, att_past, ATT_HEADS, ATT_HEAD_DIM), 1.0),
        'cache_att_v': nrm(4, (L, DEC_BATCH, att_past, ATT_HEADS, ATT_HEAD_DIM), 1.0),
        'cache_mem_k': nrm(5, (L, DEC_BATCH, N_MEM, MEM_HEADS, MEM_HEAD_DIM), 1.0),
        'cache_mem_v': nrm(6, (L, DEC_BATCH, N_MEM, MEM_HEADS, MEM_HEAD_DIM), 1.0),
        'state_rwkv': nrm(7, (L, DEC_BATCH, RW_HEADS, RW_HEAD_DIM, RW_HEAD_DIM), 0.5),
        'state_rwkv_shift': nrm(8, (L, DEC_BATCH, 1, RW_COLS), 1.0),
        'norm_mix_g': 1.0 + nrm(9, (L, D_MODEL), 0.02),
        'w_in': nrm(10, (L, D_MODEL, D_IN), D_MODEL ** -0.5),
        'rwkv_mu': uni(11, (L, RW_COLS), 0.0, 1.0),
        'rwkv_w0': uni(12, (L, RW_WIDTH), -5.0, -1.0),
        'rwkv_w2': nrm(13, (L, DECAY_LORA, RW_WIDTH), 0.5 * DECAY_LORA ** -0.5),
        'rwkv_a0': nrm(14, (L, RW_WIDTH), 0.5),
        'rwkv_a2': nrm(15, (L, ICL_LORA, RW_WIDTH), ICL_LORA ** -0.5),
        'rwkv_g2': nrm(16, (L, GATE_LORA, RW_WIDTH), GATE_LORA ** -0.5),
        'rwkv_k_k': 0.85 + nrm(17, (L, RW_WIDTH), 0.05),
        'rwkv_k_a': 1.0 + nrm(18, (L, RW_WIDTH), 0.05),
        'rwkv_r_k': nrm(19, (L, RW_HEADS, RW_HEAD_DIM), 0.1),
        'rwkv_ln_w': 1.0 + nrm(20, (L, RW_WIDTH), 0.02),
        'rwkv_ln_b': nrm(21, (L, RW_WIDTH), 0.02),
        'w_rwkv_out': nrm(22, (L, RW_WIDTH, D_MODEL), RW_WIDTH ** -0.5),
        'att_q_g': 1.0 + nrm(23, (L, ATT_HEAD_DIM), 0.02),
        'att_k_g': 1.0 + nrm(24, (L, ATT_HEAD_DIM), 0.02),
        'att_rel_bias': nrm(25, (L, ATT_HEADS, 2 * REL_CLIP + 1), 0.5),
        'w_att_out': nrm(26, (L, ATT_WIDTH, D_MODEL), ATT_WIDTH ** -0.5),
        'mem_norm_g': 1.0 + nrm(27, (L, D_MODEL), 0.02),
        'w_mem_kv': nrm(28, (L, D_MODEL, 2 * MEM_WIDTH), D_MODEL ** -0.5),
        'mem_q_g': 1.0 + nrm(29, (L, MEM_HEAD_DIM), 0.02),
        'mem_k_g': 1.0 + nrm(30, (L, MEM_HEAD_DIM), 0.02),
        'w_mem_out': nrm(31, (L, MEM_WIDTH, D_MODEL), MEM_WIDTH ** -0.5),
        'w_o': nrm(32, (L, D_MODEL, D_MODEL), D_MODEL ** -0.5),
        'norm_ffn_g': 1.0 + nrm(33, (L, D_MODEL), 0.02),
        'w_router_group': nrm(34, (L, D_MODEL, N_GROUPS), D_MODEL ** -0.5),
        'b_router_group': nrm(35, (L, N_GROUPS), 0.01),
        'w_router_expert': nrm(36, (L, D_MODEL, N_EXPERTS), D_MODEL ** -0.5),
        'b_router_expert': nrm(37, (L, N_EXPERTS), 0.01),
        'w_exp_gate': nrm(38, (L, N_EXPERTS, D_MODEL, D_EXPERT), D_MODEL ** -0.5),
        'w_exp_up': nrm(39, (L, N_EXPERTS, D_MODEL, D_EXPERT), D_MODEL ** -0.5),
        'w_exp_down': nrm(40, (L, N_EXPERTS, D_EXPERT, D_MODEL), D_EXPERT ** -0.5),
    }


def reference(x_prompt, x_sample, mem_prompt, cache_att_k, cache_att_v, cache_mem_k, cache_mem_v,
              state_rwkv, state_rwkv_shift, norm_mix_g, w_in, rwkv_mu, rwkv_w0, rwkv_w2, rwkv_a0,
              rwkv_a2, rwkv_g2, rwkv_k_k, rwkv_k_a, rwkv_r_k, rwkv_ln_w, rwkv_ln_b, w_rwkv_out,
              att_q_g, att_k_g, att_rel_bias, w_att_out, mem_norm_g, w_mem_kv, mem_q_g, mem_k_g,
              w_mem_out, w_o, norm_ffn_g, w_router_group, b_router_group, w_router_expert,
              b_router_expert, w_exp_gate, w_exp_up, w_exp_down):
    xp, xs = x_prompt, x_sample
    Bp = xp.shape[0]
    p_att_k, p_att_v, p_S, p_sh, p_mk, p_mv = [], [], [], [], [], []
    s_att_k, s_att_v, s_S, s_sh = [], [], [], []
    for l in range(DEPTH):
        lp = dict(norm_mix_g=norm_mix_g[l], w_in=w_in[l], rwkv_mu=rwkv_mu[l], rwkv_w0=rwkv_w0[l],
                  rwkv_w2=rwkv_w2[l], rwkv_a0=rwkv_a0[l], rwkv_a2=rwkv_a2[l], rwkv_g2=rwkv_g2[l],
                  rwkv_k_k=rwkv_k_k[l], rwkv_k_a=rwkv_k_a[l], rwkv_r_k=rwkv_r_k[l],
                  rwkv_ln_w=rwkv_ln_w[l], rwkv_ln_b=rwkv_ln_b[l], w_rwkv_out=w_rwkv_out[l],
                  att_q_g=att_q_g[l], att_k_g=att_k_g[l], att_rel_bias=att_rel_bias[l],
                  w_att_out=w_att_out[l], mem_norm_g=mem_norm_g[l], w_mem_kv=w_mem_kv[l],
                  mem_q_g=mem_q_g[l], mem_k_g=mem_k_g[l], w_mem_out=w_mem_out[l], w_o=w_o[l],
                  norm_ffn_g=norm_ffn_g[l], w_router_group=w_router_group[l],
                  b_router_group=b_router_group[l], w_router_expert=w_router_expert[l],
                  b_router_expert=b_router_expert[l], w_exp_gate=w_exp_gate[l],
                  w_exp_up=w_exp_up[l], w_exp_down=w_exp_down[l])
        mk_p, mv_p = memory_kv(mem_prompt, lp)
        S0 = jnp.zeros((Bp, RW_HEADS, RW_HEAD_DIM, RW_HEAD_DIM), xp.dtype)
        sh0 = jnp.zeros((Bp, 1, RW_COLS), xp.dtype)
        xp, S_p, sh_p, k_p, v_p = mixing_sublayer(xp, sh0, S0, None, None, mk_p, mv_p, lp)
        xp = xp + hierarchical_moe(xp, lp)
        xs, S_s, sh_s, k_s, v_s = mixing_sublayer(xs, state_rwkv_shift[l], state_rwkv[l], cache_att_k[l],
                                                  cache_att_v[l], cache_mem_k[l], cache_mem_v[l], lp)
        xs = xs + hierarchical_moe(xs, lp)
        p_att_k.append(k_p); p_att_v.append(v_p); p_S.append(S_p); p_sh.append(sh_p)
        p_mk.append(mk_p); p_mv.append(mv_p)
        s_att_k.append(k_s); s_att_v.append(v_s); s_S.append(S_s); s_sh.append(sh_s)
    return (xp, xs,
            jnp.stack(p_att_k), jnp.stack(p_att_v), jnp.stack(p_S), jnp.stack(p_sh),
            jnp.stack(p_mk), jnp.stack(p_mv),
            jnp.stack(s_att_k), jnp.stack(s_att_v), jnp.stack(s_S), jnp.stack(s_sh))
```

```python
import functools
import math

import numpy as np
import jax
import jax.numpy as jnp
from jax import lax
from jax.experimental import pallas as pl
from jax.experimental.pallas import tpu as pltpu

F32 = jnp.float32
BF16 = jnp.bfloat16
HIGHEST = lax.Precision.HIGHEST

D_MODEL = 4096
SEQ = 8192
DEC_BATCH = 16
DEC_SEQ = 16
PAST_LEN = 2048
N_TOK = SEQ + DEC_BATCH * DEC_SEQ
CHUNK = 64
N_PREV_CHUNKS = 8
RW_HEADS = 32
RW_HEAD_DIM = 64
RW_WIDTH = 2048
DECAY_LORA = 64
ICL_LORA = 64
GATE_LORA = 256
RW_LN_EPS = 64e-5
RW_COLS = 3 * RW_WIDTH + DECAY_LORA + ICL_LORA + GATE_LORA
RW_COLS_PAD = 6656
ATT_HEADS = 16
ATT_HEAD_DIM = 128
ATT_WIDTH = 2048
ATT_PAST = 512
REL_CLIP = 256
N_MEM = 256
MEM_HEADS = 4
MEM_HEAD_DIM = 512
MEM_WIDTH = 2048
N_GROUPS = 8
EXPERTS_PER_GROUP = 8
N_EXPERTS = 64
TOP_K = 2
D_EXPERT = 1024
NORM_EPS = 1e-6
NEG_INF = -1e30

LANES = 128
VMEM_LIMIT = 48 * 1024 * 1024
MOE_VMEM_LIMIT = 54 * 1024 * 1024
MOE_ROWS = 256
MOE_FCHUNK = 256


def _params(sem, limit=VMEM_LIMIT):
    return pltpu.CompilerParams(dimension_semantics=sem, vmem_limit_bytes=limit)


def _sigmoid(x):
    return 1.0 / (1.0 + jnp.exp(-x))


def _rmsnorm_kernel(x_ref, g_ref, o_ref):
    x = x_ref[...]
    ms = jnp.mean(x * x, axis=-1, keepdims=True)
    o_ref[...] = (x * lax.rsqrt(ms + NORM_EPS) * g_ref[...]).astype(o_ref.dtype)


def _rmsnorm(x, g, tm, out_dtype):
    m, d = x.shape
    return pl.pallas_call(
        _rmsnorm_kernel,
        grid=(m // tm,),
        in_specs=[pl.BlockSpec((tm, d), lambda i: (i, 0)),
                  pl.BlockSpec((1, d), lambda i: (0, 0))],
        out_specs=pl.BlockSpec((tm, d), lambda i: (i, 0)),
        out_shape=jax.ShapeDtypeStruct((m, d), out_dtype),
        compiler_params=_params(("parallel",)),
        name="rmsnorm",
    )(x, g.reshape(1, d))


def _mm_plain_kernel(x_ref, w_ref, o_ref):
    o_ref[...] = jnp.dot(x_ref[...], w_ref[...], preferred_element_type=F32).astype(o_ref.dtype)


def _mm_headnorm_kernel(x_ref, w_ref, g_ref, o_ref, *, head, n_norm_tiles):
    acc = jnp.dot(x_ref[...], w_ref[...], preferred_element_type=F32)
    j = pl.program_id(0)
    tn = acc.shape[1]

    @pl.when(j < n_norm_tiles)
    def _():
        for c in range(tn // head):
            blk = acc[:, c * head:(c + 1) * head]
            ms = jnp.mean(blk * blk, axis=-1, keepdims=True)
            o_ref[:, c * head:(c + 1) * head] = (
                blk * lax.rsqrt(ms + NORM_EPS) * g_ref[:, c * head:(c + 1) * head]).astype(o_ref.dtype)

    @pl.when(j >= n_norm_tiles)
    def _():
        o_ref[...] = acc.astype(o_ref.dtype)


def _mm_sigmoid_kernel(x_ref, w_ref, o_ref):
    acc = jnp.dot(x_ref[...], w_ref[...], preferred_element_type=F32)
    o_ref[...] = _sigmoid(acc).astype(o_ref.dtype)


def _mm_residual_kernel(x_ref, w_ref, r_ref, o_ref):
    o_ref[...] = r_ref[...] + jnp.dot(x_ref[...], w_ref[...], preferred_element_type=F32)


def _matmul(body, x, w, extras, extra_specs, *, tm, tn, out_dtype, name):
    m, k = x.shape
    n = w.shape[1]
    return pl.pallas_call(
        body,
        grid=(n // tn, m // tm),
        in_specs=[pl.BlockSpec((tm, k), lambda j, i: (i, 0)),
                  pl.BlockSpec((k, tn), lambda j, i: (0, j))] + list(extra_specs),
        out_specs=pl.BlockSpec((tm, tn), lambda j, i: (i, j)),
        out_shape=jax.ShapeDtypeStruct((m, n), out_dtype),
        compiler_params=_params(("parallel", "parallel")),
        name=name,
    )(x, w, *extras)


def _merge_kernel(a0_ref, a1_ref, a2_ref, w0_ref, w1_ref, w2_ref, g0_ref, g1_ref, g2_ref, o_ref):
    y0 = jnp.dot(a0_ref[...], w0_ref[...], preferred_element_type=F32)
    y1 = jnp.dot(a1_ref[...], w1_ref[...], preferred_element_type=F32)
    y2 = jnp.dot(a2_ref[...], w2_ref[...], preferred_element_type=F32)
    merged = (g0_ref[...].astype(F32) * y0 + g1_ref[...].astype(F32) * y1
              + g2_ref[...].astype(F32) * y2)
    o_ref[...] = merged.astype(o_ref.dtype)


def _gated_merge(a_rw, a_att, a_mem, w_rw, w_att, w_mem, gate, tm, tn):
    m, k = a_rw.shape
    n = w_rw.shape[1]
    nj = n // tn
    a_spec = pl.BlockSpec((tm, k), lambda j, i: (i, 0))
    w_spec = pl.BlockSpec((k, tn), lambda j, i: (0, j))
    g_specs = [pl.BlockSpec((tm, tn), functools.partial(lambda j, i, b: (i, b * nj + j), b=b))
               for b in range(3)]
    return pl.pallas_call(
        _merge_kernel,
        grid=(nj, m // tm),
        in_specs=[a_spec, a_spec, a_spec, w_spec, w_spec, w_spec] + g_specs,
        out_specs=pl.BlockSpec((tm, tn), lambda j, i: (i, j)),
        out_shape=jax.ShapeDtypeStruct((m, n), BF16),
        compiler_params=_params(("parallel", "parallel")),
        name="gated_merge",
    )(a_rw, a_att, a_mem, w_rw, w_att, w_mem, gate, gate, gate)


def _half_ones():
    qi = lax.broadcasted_iota(jnp.int32, (LANES, LANES), 0) // RW_HEAD_DIM
    qj = lax.broadcasted_iota(jnp.int32, (LANES, LANES), 1) // RW_HEAD_DIM
    return qi == qj


def _rwkv_prep_kernel(z_ref, zp_ref, sh_ref, mu_ref, w0_ref, a0_ref, kk_ref, ka_ref,
                      w2_ref, a2_ref, g2_ref,
                      r_o, w_o, k_o, v_o, kk_o, b_o, g_o, *, tiles_per_seq):
    i = pl.program_id(0)
    first = (i % tiles_per_seq) == 0
    z = z_ref[...]
    prev_row = jnp.where(first, sh_ref[...], zp_ref[7:8, :])
    rows = lax.broadcasted_iota(jnp.int32, z.shape, 0)
    z_prev = jnp.where(rows == 0, prev_row, pltpu.roll(z, 1, 0))
    zs = z + (z_prev - z) * mu_ref[...]

    c1, c2, c3 = RW_WIDTH, 2 * RW_WIDTH, 3 * RW_WIDTH
    c4, c5 = c3 + DECAY_LORA, c3 + DECAY_LORA + ICL_LORA
    r = zs[:, :c1]
    k = zs[:, c1:c2]
    v = zs[:, c2:c3]
    wl = w0_ref[...] + jnp.dot(jnp.tanh(zs[:, c3:c4]), w2_ref[...], precision=HIGHEST,
                               preferred_element_type=F32)
    decay = jnp.exp(-math.exp(-0.5) * _sigmoid(wl))
    a = _sigmoid(a0_ref[...] + jnp.dot(zs[:, c4:c5], a2_ref[...], precision=HIGHEST,
                                       preferred_element_type=F32))
    g = jnp.dot(_sigmoid(zs[:, c5:]), g2_ref[...], precision=HIGHEST, preferred_element_type=F32)

    q = _half_ones().astype(F32)
    kk = k * kk_ref[...]
    for c in range(RW_WIDTH // LANES):
        sl = slice(c * LANES, (c + 1) * LANES)
        blk = kk[:, sl]
        ss = jnp.dot(blk * blk, q, precision=HIGHEST, preferred_element_type=F32)
        blk = blk / jnp.maximum(jnp.sqrt(ss), 1e-12)
        kk_o[:, sl] = blk
        b_o[:, sl] = blk * a[:, sl]
    r_o[...] = r
    w_o[...] = decay
    k_o[...] = k * (1.0 + (a - 1.0) * ka_ref[...])
    v_o[...] = v
    g_o[...] = g


def _rwkv_prep(z_rw, shift, lp, *, row0, n_seq, seq_len, tt):
    tiles_per_seq = seq_len // tt
    n_tiles = n_seq * tiles_per_seq
    off = row0 // tt
    off8 = row0 // 8
    row = lambda a: a.reshape(1, -1)
    full = lambda shape: pl.BlockSpec(shape, lambda i: (0, 0))
    out_sd = jax.ShapeDtypeStruct((n_seq * seq_len, RW_WIDTH), F32)
    o_spec = pl.BlockSpec((tt, RW_WIDTH), lambda i: (i, 0))
    return pl.pallas_call(
        functools.partial(_rwkv_prep_kernel, tiles_per_seq=tiles_per_seq),
        grid=(n_tiles,),
        in_specs=[
            pl.BlockSpec((tt, RW_COLS), lambda i: (off + i, 0)),
            pl.BlockSpec((8, RW_COLS), lambda i: (jnp.maximum(off8 + i * (tt // 8) - 1, 0), 0)),
            pl.BlockSpec((None, 1, RW_COLS), lambda i: (i // tiles_per_seq, 0, 0)),
            full((1, RW_COLS)), full((1, RW_WIDTH)), full((1, RW_WIDTH)), full((1, RW_WIDTH)),
            full((1, RW_WIDTH)), full((DECAY_LORA, RW_WIDTH)), full((ICL_LORA, RW_WIDTH)),
            full((GATE_LORA, RW_WIDTH)),
        ],
        out_specs=[o_spec] * 7,
        out_shape=[out_sd] * 7,
        compiler_params=_params(("parallel",)),
        name="rwkv_prep",
    )(z_rw, z_rw, shift, row(lp['rwkv_mu']), row(lp['rwkv_w0']), row(lp['rwkv_a0']),
      row(lp['rwkv_k_k']), row(lp['rwkv_k_a']), lp['rwkv_w2'], lp['rwkv_a2'], lp['rwkv_g2'])


def _rwkv_scan_kernel(r_ref, w_ref, k_ref, v_ref, kk_ref, b_ref, s0_ref, y_ref, st_ref, s_scr,
                      *, tt, pairs):
    tb = pl.program_id(2)
    n_tb = pl.num_programs(2)
    lane = lax.broadcasted_iota(jnp.int32, (RW_HEAD_DIM, LANES), 1)
    sub = lax.broadcasted_iota(jnp.int32, (RW_HEAD_DIM, LANES), 0)
    diag = (lane % RW_HEAD_DIM) == sub
    q = _half_ones().astype(BF16)

    def segsum(x):
        hi = x.astype(BF16)
        lo = (x - hi.astype(F32)).astype(BF16)
        return (jnp.dot(hi, q, preferred_element_type=F32)
                + jnp.dot(lo, q, preferred_element_type=F32))

    @pl.when(tb == 0)
    def _():
        for p in range(pairs):
            s_scr[p] = jnp.concatenate([s0_ref[0, 2 * p], s0_ref[0, 2 * p + 1]], axis=1)

    def step8(t8, states):
        rows = pl.ds(pl.multiple_of(t8 * 8, 8), 8)
        out = []
        for p in range(pairs):
            sl = pl.ds(p * LANES, LANES)
            s = states[p]
            kk8, w8, k8, v8, b8, r8 = (ref[rows, sl] for ref in
                                       (kk_ref, w_ref, k_ref, v_ref, b_ref, r_ref))
            ys = []
            for i in range(8):
                row = lambda a: a[i:i + 1, :]
                sa = segsum(s * row(kk8))
                v_col = segsum(jnp.where(diag, row(v8), 0.0))
                s = s * row(w8) - sa * row(b8) + v_col * row(k8)
                y_rep = segsum(s * row(r8))
                ys.append(jnp.sum(jnp.where(diag, y_rep, 0.0), axis=0, keepdims=True))
            y_ref[rows, sl] = jnp.concatenate(ys, axis=0)
            out.append(s)
        return tuple(out)

    final = lax.fori_loop(0, tt // 8, step8, tuple(s_scr[p] for p in range(pairs)))
    for p in range(pairs):
        s_scr[p] = final[p]

    @pl.when(tb == n_tb - 1)
    def _():
        for p in range(pairs):
            st_ref[0, 2 * p] = final[p][:, :RW_HEAD_DIM]
            st_ref[0, 2 * p + 1] = final[p][:, RW_HEAD_DIM:]


def _rwkv_scan(r, w, k, v, kk, b, s0, *, n_seq, seq_len, tt, pairs):
    n_tb = seq_len // tt
    width = pairs * LANES
    heads = 2 * pairs
    t_spec = pl.BlockSpec((tt, width), lambda s, g, t: (s * n_tb + t, g))
    s_spec = pl.BlockSpec((1, heads, RW_HEAD_DIM, RW_HEAD_DIM), lambda s, g, t: (s, g, 0, 0))
    return pl.pallas_call(
        functools.partial(_rwkv_scan_kernel, tt=tt, pairs=pairs),
        grid=(n_seq, RW_WIDTH // width, n_tb),
        in_specs=[t_spec] * 6 + [s_spec],
        out_specs=[t_spec, s_spec],
        out_shape=[jax.ShapeDtypeStruct((n_seq * seq_len, RW_WIDTH), F32),
                   jax.ShapeDtypeStruct((n_seq, RW_HEADS, RW_HEAD_DIM, RW_HEAD_DIM), F32)],
        scratch_shapes=[pltpu.VMEM((pairs, RW_HEAD_DIM, LANES), F32)],
        compiler_params=_params(("parallel", "parallel", "arbitrary")),
        name="rwkv_scan",
    )(r, w, k, v, kk, b, s0)


def _rwkv_post_kernel(y_ref, r_ref, k_ref, v_ref, g_ref, lnw_ref, lnb_ref, rk_ref, o_ref):
    q = _half_ones().astype(F32)
    seg = lambda x: jnp.dot(x, q, precision=HIGHEST, preferred_element_type=F32)
    y = y_ref[...]
    mean = seg(y) * (1.0 / RW_HEAD_DIM)
    d = y - mean
    var = seg(d * d) * (1.0 / RW_HEAD_DIM)
    yn = d * lax.rsqrt(var + RW_LN_EPS) * lnw_ref[...] + lnb_ref[...]
    bonus = seg(r_ref[...] * k_ref[...] * rk_ref[...]) * v_ref[...]
    o_ref[...] = ((yn + bonus) * g_ref[...]).astype(o_ref.dtype)


def _rwkv_post(y, r, k, v, g, lp, tt):
    m = y.shape[0]
    row = lambda a: a.reshape(1, -1)
    t_spec = pl.BlockSpec((tt, LANES), lambda i, c: (i, c))
    p_spec = pl.BlockSpec((1, LANES), lambda i, c: (0, c))
    return pl.pallas_call(
        _rwkv_post_kernel,
        grid=(m // tt, RW_WIDTH // LANES),
        in_specs=[t_spec] * 5 + [p_spec] * 3,
        out_specs=t_spec,
        out_shape=jax.ShapeDtypeStruct((m, RW_WIDTH), BF16),
        compiler_params=_params(("parallel", "parallel")),
        name="rwkv_post",
    )(y, r, k, v, g, row(lp['rwkv_ln_w']), row(lp['rwkv_ln_b']), row(lp['rwkv_r_k']))


BAND_TQ = 512


def _softmax2(s1, s2):
    m = jnp.maximum(jnp.max(s1, axis=-1, keepdims=True), jnp.max(s2, axis=-1, keepdims=True))
    p1 = jnp.exp(s1 - m)
    p2 = jnp.exp(s2 - m)
    denom = jnp.sum(p1, axis=-1, keepdims=True) + jnp.sum(p2, axis=-1, keepdims=True)
    return p1, p2, denom


def _dot_nt(a, b):
    return lax.dot_general(a, b, (((1,), (1,)), ((), ())), preferred_element_type=F32)


def _band_prompt_kernel(q_ref, kp_ref, kc_ref, vp_ref, vc_ref, bias_ref, o_ref):
    i = pl.program_id(1)
    scale = ATT_HEAD_DIM ** -0.5
    q = q_ref[...].astype(BF16)
    s1 = _dot_nt(q, kp_ref[...].astype(BF16)) * scale + bias_ref[:, :BAND_TQ]
    s2 = _dot_nt(q, kc_ref[...].astype(BF16)) * scale + bias_ref[:, BAND_TQ:]
    s1 = jnp.where(i == 0, NEG_INF, s1)
    p1, p2, denom = _softmax2(s1, s2)
    o = (jnp.dot(p1.astype(BF16), vp_ref[...].astype(BF16), preferred_element_type=F32)
         + jnp.dot(p2.astype(BF16), vc_ref[...].astype(BF16), preferred_element_type=F32))
    o_ref[...] = (o / denom).astype(o_ref.dtype)


def _band_prompt(qkv, bias):
    nq = SEQ // BAND_TQ
    prev = lambda h, i: jnp.maximum(i - 1, 0)
    blk = (BAND_TQ, ATT_HEAD_DIM)
    return pl.pallas_call(
        _band_prompt_kernel,
        grid=(ATT_HEADS, nq),
        in_specs=[
            pl.BlockSpec(blk, lambda h, i: (i, h)),
            pl.BlockSpec(blk, lambda h, i: (prev(h, i), ATT_HEADS + h)),
            pl.BlockSpec(blk, lambda h, i: (i, ATT_HEADS + h)),
            pl.BlockSpec(blk, lambda h, i: (prev(h, i), 2 * ATT_HEADS + h)),
            pl.BlockSpec(blk, lambda h, i: (i, 2 * ATT_HEADS + h)),
            pl.BlockSpec((None, BAND_TQ, 2 * BAND_TQ), lambda h, i: (h, 0, 0)),
        ],
        out_specs=pl.BlockSpec(blk, lambda h, i: (i, h)),
        out_shape=jax.ShapeDtypeStruct((SEQ, ATT_WIDTH), BF16),
        compiler_params=_params(("parallel", "parallel")),
        name="band_attention_prompt",
    )(qkv, qkv, qkv, qkv, qkv, bias)


def _band_prompt_bias(rel_bias):
    iq = np.arange(BAND_TQ)[:, None]
    ik = np.arange(2 * BAND_TQ)[None, :]
    rel = np.clip(iq - (ik - BAND_TQ), -REL_CLIP, REL_CLIP) + REL_CLIP
    qc = iq // CHUNK + N_PREV_CHUNKS
    kc = ik // CHUNK
    mask = (kc <= qc) & (kc >= qc - N_PREV_CHUNKS)
    table = jnp.take(rel_bias, jnp.asarray(rel.reshape(-1)), axis=1).reshape(
        ATT_HEADS, BAND_TQ, 2 * BAND_TQ)
    return jnp.where(jnp.asarray(mask)[None], table, NEG_INF).astype(F32)


def _band_sample_kernel(q_ref, k_ref, v_ref, kc_ref, vc_ref, b1_ref, b2_ref, o_ref):
    scale = ATT_HEAD_DIM ** -0.5
    for h in range(ATT_HEADS):
        sl = slice(h * ATT_HEAD_DIM, (h + 1) * ATT_HEAD_DIM)
        q = q_ref[:, sl].astype(BF16)
        s1 = _dot_nt(q, kc_ref[:, sl].astype(BF16)) * scale + b1_ref[h]
        s2 = _dot_nt(q, k_ref[:, sl].astype(BF16)) * scale + b2_ref[h]
        p1, p2, denom = _softmax2(s1, s2)
        o = (jnp.dot(p1.astype(BF16), vc_ref[:, sl].astype(BF16), preferred_element_type=F32)
             + jnp.dot(p2.astype(BF16), v_ref[:, sl].astype(BF16), preferred_element_type=F32))
        o_ref[:, sl] = (o / denom).astype(o_ref.dtype)


def _band_sample(qkv, cache_k, cache_v, bias_past, bias_new):
    row0 = SEQ // DEC_SEQ
    new = lambda c: pl.BlockSpec((DEC_SEQ, ATT_WIDTH), lambda b: (row0 + b, c))
    cache = pl.BlockSpec((None, ATT_PAST, ATT_WIDTH), lambda b: (b, 0, 0))
    return pl.pallas_call(
        _band_sample_kernel,
        grid=(DEC_BATCH,),
        in_specs=[new(0), new(1), new(2), cache, cache,
                  pl.BlockSpec((ATT_HEADS, DEC_SEQ, ATT_PAST), lambda b: (0, 0, 0)),
                  pl.BlockSpec((ATT_HEADS, DEC_SEQ, DEC_SEQ), lambda b: (0, 0, 0))],
        out_specs=pl.BlockSpec((DEC_SEQ, ATT_WIDTH), lambda b: (b, 0)),
        out_shape=jax.ShapeDtypeStruct((DEC_BATCH * DEC_SEQ, ATT_WIDTH), BF16),
        compiler_params=_params(("parallel",)),
        name="band_attention_sample",
    )(qkv, qkv, qkv, cache_k, cache_v, bias_past, bias_new)


def _band_sample_bias(rel_bias):
    q_pos = PAST_LEN + np.arange(DEC_SEQ)[:, None]
    k_pos = PAST_LEN - ATT_PAST + np.arange(ATT_PAST + DEC_SEQ)[None, :]
    rel = np.clip(q_pos - k_pos, -REL_CLIP, REL_CLIP) + REL_CLIP
    qc, kc = q_pos // CHUNK, k_pos // CHUNK
    mask = (k_pos >= 0) & (kc <= qc) & (kc >= qc - N_PREV_CHUNKS)
    table = jnp.take(rel_bias, jnp.asarray(rel.reshape(-1)), axis=1).reshape(
        ATT_HEADS, DEC_SEQ, ATT_PAST + DEC_SEQ)
    table = jnp.where(jnp.asarray(mask)[None], table, NEG_INF).astype(F32)
    return table[:, :, :ATT_PAST], table[:, :, ATT_PAST:]


def _mem_attn_kernel(q_ref, k_ref, v_ref, o_ref):
    s = _dot_nt(q_ref[...], k_ref[...].astype(BF16)) * (MEM_HEAD_DIM ** -0.5)
    m = jnp.max(s, axis=-1, keepdims=True)
    p = jnp.exp(s - m)
    denom = jnp.sum(p, axis=-1, keepdims=True)
    o = jnp.dot(p.astype(BF16), v_ref[...].astype(BF16), preferred_element_type=F32)
    o_ref[...] = (o / denom).astype(o_ref.dtype)


def _mem_attn(q_mem, mem_k, mem_v, *, row0, n_rows, tq, rows_per_mem):
    off = row0 // tq
    kv_spec = pl.BlockSpec((None, N_MEM, MEM_HEAD_DIM), lambda h, i: ((i * tq) // rows_per_mem, 0, h))
    return pl.pallas_call(
        _mem_attn_kernel,
        grid=(MEM_HEADS, n_rows // tq),
        in_specs=[pl.BlockSpec((tq, MEM_HEAD_DIM), lambda h, i: (off + i, h)), kv_spec, kv_spec],
        out_specs=pl.BlockSpec((tq, MEM_HEAD_DIM), lambda h, i: (i, h)),
        out_shape=jax.ShapeDtypeStruct((n_rows, MEM_WIDTH), BF16),
        compiler_params=_params(("parallel", "parallel")),
        name="memory_attention",
    )(q_mem, mem_k, mem_v)


ROUTER_COLS = 128


def _norm_router_kernel(x_ref, g_ref, wr_ref, br_ref, h_ref, logit_ref):
    x = x_ref[...]
    ms = jnp.mean(x * x, axis=-1, keepdims=True)
    h = x * lax.rsqrt(ms + NORM_EPS) * g_ref[...]
    h_ref[...] = h.astype(h_ref.dtype)
    logit_ref[...] = jnp.dot(h, wr_ref[...], precision=HIGHEST,
                             preferred_element_type=F32) + br_ref[...]


def _norm_router(x, g, w_router, b_router, tm):
    m, d = x.shape
    return pl.pallas_call(
        _norm_router_kernel,
        grid=(m // tm,),
        in_specs=[pl.BlockSpec((tm, d), lambda i: (i, 0)),
                  pl.BlockSpec((1, d), lambda i: (0, 0)),
                  pl.BlockSpec((d, ROUTER_COLS), lambda i: (0, 0)),
                  pl.BlockSpec((1, ROUTER_COLS), lambda i: (0, 0))],
        out_specs=[pl.BlockSpec((tm, d), lambda i: (i, 0)),
                   pl.BlockSpec((tm, ROUTER_COLS), lambda i: (i, 0))],
        out_shape=[jax.ShapeDtypeStruct((m, d), BF16),
                   jax.ShapeDtypeStruct((m, ROUTER_COLS), F32)],
        compiler_params=_params(("parallel",)),
        name="ffn_norm_router",
    )(x, g.reshape(1, d), w_router, b_router)


def _expert_kernel(blk_e_ref, blk_ok_ref, x_ref, wg_ref, wu_ref, wd_ref, rw_ref, o_ref):
    b = pl.program_id(0)
    f = pl.program_id(1)
    ok = blk_ok_ref[b] > 0

    @pl.when(ok)
    def _():
        x = x_ref[...]
        gate = jnp.dot(x, wg_ref[...].astype(BF16), preferred_element_type=F32)
        up = jnp.dot(x, wu_ref[...].astype(BF16), preferred_element_type=F32)
        hidden = (gate * _sigmoid(gate) * up).astype(BF16)
        part = jnp.dot(hidden, wd_ref[...].astype(BF16), preferred_element_type=F32)

        @pl.when(f == 0)
        def _():
            o_ref[...] = part

        @pl.when(f > 0)
        def _():
            o_ref[...] += part

        @pl.when(f == pl.num_programs(1) - 1)
        def _():
            o_ref[...] = o_ref[...] * rw_ref[...]

    @pl.when(jnp.logical_not(ok) & (f == 0))
    def _():
        o_ref[...] = jnp.zeros_like(o_ref)


def _expert_ffn(xs, row_w, blk_e, blk_ok, w_gate, w_up, w_down):
    rows = xs.shape[0]
    nb = rows // MOE_ROWS
    nf = D_EXPERT // MOE_FCHUNK
    fidx = lambda b, f, ok: jnp.where(ok[b] > 0, f, nf - 1)
    grid_spec = pltpu.PrefetchScalarGridSpec(
        num_scalar_prefetch=2,
        grid=(nb, nf),
        in_specs=[
            pl.BlockSpec((MOE_ROWS, D_MODEL), lambda b, f, e, ok: (b, 0)),
            pl.BlockSpec((None, D_MODEL, MOE_FCHUNK), lambda b, f, e, ok: (e[b], 0, fidx(b, f, ok))),
            pl.BlockSpec((None, D_MODEL, MOE_FCHUNK), lambda b, f, e, ok: (e[b], 0, fidx(b, f, ok))),
            pl.BlockSpec((None, MOE_FCHUNK, D_MODEL), lambda b, f, e, ok: (e[b], fidx(b, f, ok), 0)),
            pl.BlockSpec((MOE_ROWS, 1), lambda b, f, e, ok: (b, 0)),
        ],
        out_specs=pl.BlockSpec((MOE_ROWS, D_MODEL), lambda b, f, e, ok: (b, 0)),
    )
    return pl.pallas_call(
        _expert_kernel,
        grid_spec=grid_spec,
        out_shape=jax.ShapeDtypeStruct((rows, D_MODEL), F32),
        compiler_params=_params(("arbitrary", "arbitrary"), MOE_VMEM_LIMIT),
        name="expert_ffn",
    )(blk_e, blk_ok, xs, w_gate, w_up, w_down, row_w)


def _route(logits):
    n = logits.shape[0]
    g_prob = jax.nn.softmax(logits[:, :N_GROUPS], axis=-1)
    pg, gidx = lax.top_k(g_prob, 1)
    e_logits = logits[:, N_GROUPS:N_GROUPS + N_EXPERTS].reshape(n, N_GROUPS, EXPERTS_PER_GROUP)
    e_sel = e_logits[jnp.arange(n), gidx[:, 0]]
    pe, eidx = lax.top_k(jax.nn.softmax(e_sel, axis=-1), TOP_K)
    weights = pg * pe / jnp.sum(pe, axis=-1, keepdims=True)
    return gidx * EXPERTS_PER_GROUP + eidx, weights


def _moe_layout(expert_id, weights):
    n = expert_id.shape[0]
    a = n * TOP_K
    n_blocks = a // MOE_ROWS + N_EXPERTS
    rows = n_blocks * MOE_ROWS
    flat_e = expert_id.reshape(a)
    flat_tok = jnp.arange(a, dtype=jnp.int32) // TOP_K
    order = jnp.argsort(flat_e)
    sorted_e = flat_e[order]
    counts = jnp.bincount(flat_e, length=N_EXPERTS)
    start = jnp.cumsum(counts) - counts
    padded = (counts + MOE_ROWS - 1) // MOE_ROWS * MOE_ROWS
    pad_end = jnp.cumsum(padded)
    pad_start = pad_end - padded
    dest = (pad_start[sorted_e] + (jnp.arange(a, dtype=jnp.int32) - start[sorted_e])).astype(jnp.int32)
    row_tok = jnp.zeros((rows,), jnp.int32).at[dest].set(flat_tok[order])
    row_w = jnp.zeros((rows,), F32).at[dest].set(weights.reshape(a)[order])
    pos = jnp.zeros((a,), jnp.int32).at[order].set(dest).reshape(n, TOP_K)
    blk_start = jnp.arange(n_blocks, dtype=jnp.int32) * MOE_ROWS
    blk_ok = (blk_start < pad_end[-1]).astype(jnp.int32)
    blk_e = jnp.minimum(jnp.searchsorted(pad_end, blk_start, side='right'), N_EXPERTS - 1)
    last_e = jnp.max(jnp.where(counts > 0, jnp.arange(N_EXPERTS), 0))
    blk_e = jnp.where(blk_ok > 0, blk_e, last_e).astype(jnp.int32)
    return row_tok, row_w, pos, blk_e, blk_ok


def kernel(x_prompt, x_sample, mem_prompt, cache_att_k, cache_att_v, cache_mem_k, cache_mem_v, state_rwkv, state_rwkv_shift, norm_mix_g, w_in, rwkv_mu, rwkv_w0, rwkv_w2, rwkv_a0, rwkv_a2, rwkv_g2, rwkv_k_k, rwkv_k_a, rwkv_r_k, rwkv_ln_w, rwkv_ln_b, w_rwkv_out, att_q_g, att_k_g, att_rel_bias, w_att_out, mem_norm_g, w_mem_kv, mem_q_g, mem_k_g, w_mem_out, w_o, norm_ffn_g, w_router_group, b_router_group, w_router_expert, b_router_expert, w_exp_gate, w_exp_up, w_exp_down):
    lp = dict(rwkv_mu=rwkv_mu[0], rwkv_w0=rwkv_w0[0], rwkv_w2=rwkv_w2[0], rwkv_a0=rwkv_a0[0],
              rwkv_a2=rwkv_a2[0], rwkv_g2=rwkv_g2[0], rwkv_k_k=rwkv_k_k[0], rwkv_k_a=rwkv_k_a[0],
              rwkv_r_k=rwkv_r_k[0], rwkv_ln_w=rwkv_ln_w[0], rwkv_ln_b=rwkv_ln_b[0])
    n_samp = DEC_BATCH * DEC_SEQ
    x = jnp.concatenate([x_prompt.reshape(SEQ, D_MODEL), x_sample.reshape(n_samp, D_MODEL)], axis=0)

    h = _rmsnorm(x, norm_mix_g[0], 264, BF16)
    w_in0 = w_in[0]
    c1 = RW_COLS
    c2 = c1 + 3 * ATT_WIDTH
    c3 = c2 + MEM_WIDTH
    w_rw = jnp.pad(w_in0[:, :c1].astype(BF16), ((0, 0), (0, RW_COLS_PAD - RW_COLS)))
    z_rw = _matmul(_mm_plain_kernel, h, w_rw, (), (), tm=1056, tn=512, out_dtype=F32, name="proj_rwkv")

    qk_gain = jnp.concatenate([jnp.tile(att_q_g[0], ATT_HEADS), jnp.tile(att_k_g[0], ATT_HEADS),
                               jnp.ones((ATT_WIDTH,), F32)]).reshape(1, 3 * ATT_WIDTH)
    qkv = _matmul(functools.partial(_mm_headnorm_kernel, head=ATT_HEAD_DIM, n_norm_tiles=8),
                  h, w_in0[:, c1:c2].astype(BF16), (qk_gain,),
                  (pl.BlockSpec((1, 512), lambda j, i: (0, j)),),
                  tm=1056, tn=512, out_dtype=F32, name="proj_qkv")
    q_mem = _matmul(functools.partial(_mm_headnorm_kernel, head=MEM_HEAD_DIM, n_norm_tiles=MEM_HEADS),
                    h, w_in0[:, c2:c3].astype(BF16), (jnp.tile(mem_q_g[0], MEM_HEADS).reshape(1, MEM_WIDTH),),
                    (pl.BlockSpec((1, 512), lambda j, i: (0, j)),),
                    tm=1056, tn=512, out_dtype=BF16, name="proj_mem_q")
    gate = _matmul(_mm_sigmoid_kernel, h, w_in0[:, c3:].astype(BF16), (), (),
                   tm=1056, tn=512, out_dtype=BF16, name="proj_gate")

    shift_p = jnp.zeros((1, 1, RW_COLS), F32)
    prep_p = _rwkv_prep(z_rw, shift_p, lp, row0=0, n_seq=1, seq_len=SEQ, tt=128)
    prep_s = _rwkv_prep(z_rw, state_rwkv_shift[0], lp, row0=SEQ, n_seq=DEC_BATCH, seq_len=DEC_SEQ,
                        tt=DEC_SEQ)
    s0_p = jnp.zeros((1, RW_HEADS, RW_HEAD_DIM, RW_HEAD_DIM), F32)
    rw_parts = []
    states = []
    for (r, w, k, v, kk, b, g), s0, n_seq, seq_len, tt in (
            (prep_p, s0_p, 1, SEQ, 256), (prep_s, state_rwkv[0], DEC_BATCH, DEC_SEQ, DEC_SEQ)):
        y, s_new = _rwkv_scan(r, w, k, v, kk, b, s0, n_seq=n_seq, seq_len=seq_len, tt=tt, pairs=4)
        rw_parts.append(_rwkv_post(y, r, k, v, g, lp, 256))
        states.append(s_new)
    a_rw = jnp.concatenate(rw_parts, axis=0)

    bias_p = _band_prompt_bias(att_rel_bias[0])
    bias_s1, bias_s2 = _band_sample_bias(att_rel_bias[0])
    att_p = _band_prompt(qkv, bias_p)
    att_s = _band_sample(qkv, cache_att_k[0].reshape(DEC_BATCH, ATT_PAST, ATT_WIDTH),
                         cache_att_v[0].reshape(DEC_BATCH, ATT_PAST, ATT_WIDTH), bias_s1, bias_s2)
    a_att = jnp.concatenate([att_p, att_s], axis=0)

    mem_h = _rmsnorm(mem_prompt.reshape(N_MEM, D_MODEL), mem_norm_g[0], N_MEM, BF16)
    w_kv = w_mem_kv[0]
    mk_p = _matmul(functools.partial(_mm_headnorm_kernel, head=MEM_HEAD_DIM, n_norm_tiles=MEM_HEADS),
                   mem_h, w_kv[:, :MEM_WIDTH].astype(BF16),
                   (jnp.tile(mem_k_g[0], MEM_HEADS).reshape(1, MEM_WIDTH),),
                   (pl.BlockSpec((1, 512), lambda j, i: (0, j)),),
                   tm=N_MEM, tn=512, out_dtype=F32, name="mem_k")
    mv_p = _matmul(_mm_plain_kernel, mem_h, w_kv[:, MEM_WIDTH:].astype(BF16), (), (),
                   tm=N_MEM, tn=512, out_dtype=F32, name="mem_v")
    mem_p = _mem_attn(q_mem, mk_p.reshape(1, N_MEM, MEM_WIDTH), mv_p.reshape(1, N_MEM, MEM_WIDTH),
                      row0=0, n_rows=SEQ, tq=512, rows_per_mem=SEQ)
    mem_s = _mem_attn(q_mem, cache_mem_k[0].reshape(DEC_BATCH, N_MEM, MEM_WIDTH),
                      cache_mem_v[0].reshape(DEC_BATCH, N_MEM, MEM_WIDTH),
                      row0=SEQ, n_rows=n_samp, tq=DEC_SEQ, rows_per_mem=DEC_SEQ)
    a_mem = jnp.concatenate([mem_p, mem_s], axis=0)

    merged = _gated_merge(a_rw, a_att, a_mem, w_rwkv_out[0].astype(BF16), w_att_out[0].astype(BF16),
                          w_mem_out[0].astype(BF16), gate, 528, 512)
    x1 = _matmul(_mm_residual_kernel, merged, w_o[0].astype(BF16), (x,),
                 (pl.BlockSpec((1056, 512), lambda j, i: (i, j)),),
                 tm=1056, tn=512, out_dtype=F32, name="proj_out")

    w_router = jnp.zeros((D_MODEL, ROUTER_COLS), F32)
    w_router = w_router.at[:, :N_GROUPS].set(w_router_group[0])
    w_router = w_router.at[:, N_GROUPS:N_GROUPS + N_EXPERTS].set(w_router_expert[0])
    b_router = jnp.zeros((1, ROUTER_COLS), F32)
    b_router = b_router.at[0, :N_GROUPS].set(b_router_group[0])
    b_router = b_router.at[0, N_GROUPS:N_GROUPS + N_EXPERTS].set(b_router_expert[0])
    h2, logits = _norm_router(x1, norm_ffn_g[0], w_router, b_router, 264)
    expert_id, weights = _route(logits)
    row_tok, row_w, pos, blk_e, blk_ok = _moe_layout(expert_id, weights)
    yb = _expert_ffn(jnp.take(h2, row_tok, axis=0), row_w.reshape(-1, 1), blk_e, blk_ok,
                     w_exp_gate[0], w_exp_up[0], w_exp_down[0])
    x2 = x1 + (jnp.take(yb, pos[:, 0], axis=0) + jnp.take(yb, pos[:, 1], axis=0))

    heads = lambda t, bsz, rows: t.reshape(1, bsz, rows, ATT_HEADS, ATT_HEAD_DIM)
    k_cols = slice(ATT_WIDTH, 2 * ATT_WIDTH)
    v_cols = slice(2 * ATT_WIDTH, 3 * ATT_WIDTH)
    keep = SEQ - N_PREV_CHUNKS * CHUNK
    z_s = z_rw[SEQ:, :RW_COLS].reshape(DEC_BATCH, DEC_SEQ, RW_COLS)
    return (
        x2[:SEQ].reshape(1, SEQ, D_MODEL),
        x2[SEQ:].reshape(DEC_BATCH, DEC_SEQ, D_MODEL),
        heads(qkv[keep:SEQ, k_cols], 1, SEQ - keep),
        heads(qkv[keep:SEQ, v_cols], 1, SEQ - keep),
        states[0].reshape(1, 1, RW_HEADS, RW_HEAD_DIM, RW_HEAD_DIM),
        z_rw[SEQ - 1:SEQ, :RW_COLS].reshape(1, 1, 1, RW_COLS),
        mk_p.reshape(1, 1, N_MEM, MEM_HEADS, MEM_HEAD_DIM),
        mv_p.reshape(1, 1, N_MEM, MEM_HEADS, MEM_HEAD_DIM),
        heads(qkv[SEQ:, k_cols], DEC_BATCH, DEC_SEQ),
        heads(qkv[SEQ:, v_cols], DEC_BATCH, DEC_SEQ),
        states[1].reshape(1, DEC_BATCH, RW_HEADS, RW_HEAD_DIM, RW_HEAD_DIM),
        z_s[:, -1:, :].reshape(1, DEC_BATCH, 1, RW_COLS),
    )
```

```python
import functools
import math

import numpy as np
import jax
import jax.numpy as jnp
from jax import lax
from jax.experimental import pallas as pl
from jax.experimental.pallas import tpu as pltpu

F32 = jnp.float32
BF16 = jnp.bfloat16
HIGHEST = lax.Precision.HIGHEST

D_MODEL = 4096
SEQ = 8192
DEC_BATCH = 16
DEC_SEQ = 16
PAST_LEN = 2048
N_TOK = SEQ + DEC_BATCH * DEC_SEQ
CHUNK = 64
N_PREV_CHUNKS = 8
RW_HEADS = 32
RW_HEAD_DIM = 64
RW_WIDTH = 2048
DECAY_LORA = 64
ICL_LORA = 64
GATE_LORA = 256
RW_LN_EPS = 64e-5
RW_COLS = 3 * RW_WIDTH + DECAY_LORA + ICL_LORA + GATE_LORA
RW_COLS_PAD = 6656
ATT_HEADS = 16
ATT_HEAD_DIM = 128
ATT_WIDTH = 2048
ATT_PAST = 512
REL_CLIP = 256
N_MEM = 256
MEM_HEADS = 4
MEM_HEAD_DIM = 512
MEM_WIDTH = 2048
N_GROUPS = 8
EXPERTS_PER_GROUP = 8
N_EXPERTS = 64
TOP_K = 2
D_EXPERT = 1024
NORM_EPS = 1e-6
NEG_INF = -1e30

LANES = 128
VMEM_LIMIT = 48 * 1024 * 1024
MOE_VMEM_LIMIT = 54 * 1024 * 1024
MOE_ROWS = 256
MOE_FCHUNK = 256


def _params(sem, limit=VMEM_LIMIT):
    return pltpu.CompilerParams(dimension_semantics=sem, vmem_limit_bytes=limit)


def _sigmoid(x):
    return 1.0 / (1.0 + jnp.exp(-x))


def _rmsnorm_kernel(x_ref, g_ref, o_ref):
    x = x_ref[...]
    ms = jnp.mean(x * x, axis=-1, keepdims=True)
    o_ref[...] = (x * lax.rsqrt(ms + NORM_EPS) * g_ref[...]).astype(o_ref.dtype)


def _rmsnorm(x, g, tm, out_dtype):
    m, d = x.shape
    return pl.pallas_call(
        _rmsnorm_kernel,
        grid=(m // tm,),
        in_specs=[pl.BlockSpec((tm, d), lambda i: (i, 0)),
                  pl.BlockSpec((1, d), lambda i: (0, 0))],
        out_specs=pl.BlockSpec((tm, d), lambda i: (i, 0)),
        out_shape=jax.ShapeDtypeStruct((m, d), out_dtype),
        compiler_params=_params(("parallel",)),
        name="rmsnorm",
    )(x, g.reshape(1, d))


def _mm_plain_kernel(x_ref, w_ref, o_ref):
    o_ref[...] = jnp.dot(x_ref[...], w_ref[...], preferred_element_type=F32).astype(o_ref.dtype)


def _mm_headnorm_kernel(x_ref, w_ref, g_ref, o_ref, *, head, n_norm_tiles):
    acc = jnp.dot(x_ref[...], w_ref[...], preferred_element_type=F32)
    j = pl.program_id(0)
    tn = acc.shape[1]

    @pl.when(j < n_norm_tiles)
    def _():
        for c in range(tn // head):
            blk = acc[:, c * head:(c + 1) * head]
            ms = jnp.mean(blk * blk, axis=-1, keepdims=True)
            o_ref[:, c * head:(c + 1) * head] = (
                blk * lax.rsqrt(ms + NORM_EPS) * g_ref[:, c * head:(c + 1) * head]).astype(o_ref.dtype)

    @pl.when(j >= n_norm_tiles)
    def _():
        o_ref[...] = acc.astype(o_ref.dtype)


def _mm_sigmoid_kernel(x_ref, w_ref, o_ref):
    acc = jnp.dot(x_ref[...], w_ref[...], preferred_element_type=F32)
    o_ref[...] = _sigmoid(acc).astype(o_ref.dtype)


def _mm_residual_kernel(x_ref, w_ref, r_ref, o_ref):
    o_ref[...] = r_ref[...] + jnp.dot(x_ref[...], w_ref[...], preferred_element_type=F32)


def _matmul(body, x, w, extras, extra_specs, *, tm, tn, out_dtype, name):
    m, k = x.shape
    n = w.shape[1]
    return pl.pallas_call(
        body,
        grid=(n // tn, m // tm),
        in_specs=[pl.BlockSpec((tm, k), lambda j, i: (i, 0)),
                  pl.BlockSpec((k, tn), lambda j, i: (0, j))] + list(extra_specs),
        out_specs=pl.BlockSpec((tm, tn), lambda j, i: (i, j)),
        out_shape=jax.ShapeDtypeStruct((m, n), out_dtype),
        compiler_params=_params(("parallel", "parallel")),
        name=name,
    )(x, w, *extras)


def _merge_kernel(a0_ref, a1_ref, a2_ref, w0_ref, w1_ref, w2_ref, g0_ref, g1_ref, g2_ref, o_ref):
    y0 = jnp.dot(a0_ref[...], w0_ref[...], preferred_element_type=F32)
    y1 = jnp.dot(a1_ref[...], w1_ref[...], preferred_element_type=F32)
    y2 = jnp.dot(a2_ref[...], w2_ref[...], preferred_element_type=F32)
    merged = (g0_ref[...].astype(F32) * y0 + g1_ref[...].astype(F32) * y1
              + g2_ref[...].astype(F32) * y2)
    o_ref[...] = merged.astype(o_ref.dtype)


def _gated_merge(a_rw, a_att, a_mem, w_rw, w_att, w_mem, gate, tm, tn):
    m, k = a_rw.shape
    n = w_rw.shape[1]
    nj = n // tn
    a_spec = pl.BlockSpec((tm, k), lambda j, i: (i, 0))
    w_spec = pl.BlockSpec((k, tn), lambda j, i: (0, j))
    g_specs = [pl.BlockSpec((tm, tn), functools.partial(lambda j, i, b: (i, b * nj + j), b=b))
               for b in range(3)]
    return pl.pallas_call(
        _merge_kernel,
        grid=(nj, m // tm),
        in_specs=[a_spec, a_spec, a_spec, w_spec, w_spec, w_spec] + g_specs,
        out_specs=pl.BlockSpec((tm, tn), lambda j, i: (i, j)),
        out_shape=jax.ShapeDtypeStruct((m, n), BF16),
        compiler_params=_params(("parallel", "parallel")),
        name="gated_merge",
    )(a_rw, a_att, a_mem, w_rw, w_att, w_mem, gate, gate, gate)


def _half_ones():
    qi = lax.broadcasted_iota(jnp.int32, (LANES, LANES), 0) // RW_HEAD_DIM
    qj = lax.broadcasted_iota(jnp.int32, (LANES, LANES), 1) // RW_HEAD_DIM
    return qi == qj


def _rwkv_prep_kernel(z_ref, zp_ref, sh_ref, mu_ref, w0_ref, a0_ref, kk_ref, ka_ref,
                      w2_ref, a2_ref, g2_ref,
                      r_o, w_o, k_o, v_o, kk_o, b_o, g_o, *, tiles_per_seq):
    i = pl.program_id(0)
    first = (i % tiles_per_seq) == 0
    z = z_ref[...]
    prev_row = jnp.where(first, sh_ref[...], zp_ref[7:8, :])
    rows = lax.broadcasted_iota(jnp.int32, z.shape, 0)
    z_prev = jnp.where(rows == 0, prev_row, pltpu.roll(z, 1, 0))
    zs = z + (z_prev - z) * mu_ref[...]

    c1, c2, c3 = RW_WIDTH, 2 * RW_WIDTH, 3 * RW_WIDTH
    c4, c5 = c3 + DECAY_LORA, c3 + DECAY_LORA + ICL_LORA
    r = zs[:, :c1]
    k = zs[:, c1:c2]
    v = zs[:, c2:c3]
    wl = w0_ref[...] + jnp.dot(jnp.tanh(zs[:, c3:c4]), w2_ref[...], precision=HIGHEST,
                               preferred_element_type=F32)
    decay = jnp.exp(-math.exp(-0.5) * _sigmoid(wl))
    a = _sigmoid(a0_ref[...] + jnp.dot(zs[:, c4:c5], a2_ref[...], precision=HIGHEST,
                                       preferred_element_type=F32))
    g = jnp.dot(_sigmoid(zs[:, c5:]), g2_ref[...], precision=HIGHEST, preferred_element_type=F32)

    q = _half_ones().astype(F32)
    kk = k * kk_ref[...]
    for c in range(RW_WIDTH // LANES):
        sl = slice(c * LANES, (c + 1) * LANES)
        blk = kk[:, sl]
        ss = jnp.dot(blk * blk, q, precision=HIGHEST, preferred_element_type=F32)
        blk = blk / jnp.maximum(jnp.sqrt(ss), 1e-12)
        kk_o[:, sl] = blk
        b_o[:, sl] = blk * a[:, sl]
    r_o[...] = r
    w_o[...] = decay
    k_o[...] = k * (1.0 + (a - 1.0) * ka_ref[...])
    v_o[...] = v
    g_o[...] = g


def _rwkv_prep(z_rw, shift, lp, *, row0, n_seq, seq_len, tt):
    tiles_per_seq = seq_len // tt
    n_tiles = n_seq * tiles_per_seq
    off = row0 // tt
    off8 = row0 // 8
    row = lambda a: a.reshape(1, -1)
    full = lambda shape: pl.BlockSpec(shape, lambda i: (0, 0))
    out_sd = jax.ShapeDtypeStruct((n_seq * seq_len, RW_WIDTH), F32)
    o_spec = pl.BlockSpec((tt, RW_WIDTH), lambda i: (i, 0))
    return pl.pallas_call(
        functools.partial(_rwkv_prep_kernel, tiles_per_seq=tiles_per_seq),
        grid=(n_tiles,),
        in_specs=[
            pl.BlockSpec((tt, RW_COLS), lambda i: (off + i, 0)),
            pl.BlockSpec((8, RW_COLS), lambda i: (jnp.maximum(off8 + i * (tt // 8) - 1, 0), 0)),
            pl.BlockSpec((None, 1, RW_COLS), lambda i: (i // tiles_per_seq, 0, 0)),
            full((1, RW_COLS)), full((1, RW_WIDTH)), full((1, RW_WIDTH)), full((1, RW_WIDTH)),
            full((1, RW_WIDTH)), full((DECAY_LORA, RW_WIDTH)), full((ICL_LORA, RW_WIDTH)),
            full((GATE_LORA, RW_WIDTH)),
        ],
        out_specs=[o_spec] * 7,
        out_shape=[out_sd] * 7,
        compiler_params=_params(("parallel",)),
        name="rwkv_prep",
    )(z_rw, z_rw, shift, row(lp['rwkv_mu']), row(lp['rwkv_w0']), row(lp['rwkv_a0']),
      row(lp['rwkv_k_k']), row(lp['rwkv_k_a']), lp['rwkv_w2'], lp['rwkv_a2'], lp['rwkv_g2'])


def _rwkv_scan_kernel(r_ref, w_ref, k_ref, v_ref, kk_ref, b_ref, s0_ref, y_ref, st_ref, s_scr,
                      *, tt, pairs):
    tb = pl.program_id(2)
    n_tb = pl.num_programs(2)
    lane = lax.broadcasted_iota(jnp.int32, (RW_HEAD_DIM, LANES), 1)
    sub = lax.broadcasted_iota(jnp.int32, (RW_HEAD_DIM, LANES), 0)
    diag = (lane % RW_HEAD_DIM) == sub
    q = _half_ones().astype(BF16)

    def columns(tiles):
        hi_rows, lo_rows = [], []
        for x in tiles:
            hi = x.astype(BF16).astype(F32)
            lo = x - hi
            for s in range(8):
                hi_rows.append(jnp.where(diag, hi[s:s + 1, :], 0.0).astype(BF16))
                lo_rows.append(jnp.where(diag, lo[s:s + 1, :], 0.0).astype(BF16))
        lhs = jnp.concatenate([jnp.concatenate(hi_rows, axis=0), jnp.concatenate(lo_rows, axis=0)], axis=1)
        return jnp.dot(lhs, jnp.concatenate([q, q], axis=0), preferred_element_type=F32)

    @pl.when(tb == 0)
    def _():
        for p in range(pairs):
            s_scr[p] = jnp.concatenate([s0_ref[0, 2 * p], s0_ref[0, 2 * p + 1]], axis=1)

    def step8(t8, states):
        rows = pl.ds(pl.multiple_of(t8 * 8, 8), 8)
        lanes = [pl.ds(p * LANES, LANES) for p in range(pairs)]
        cols = [columns([ref[rows, lanes[p]] for ref in (kk_ref, w_ref, b_ref, k_ref, r_ref)])
                for p in range(pairs)]
        v8 = [v_ref[rows, lanes[p]] for p in range(pairs)]
        states = list(states)
        ys = [[] for _ in range(pairs)]
        for s in range(8):
            for p in range(pairs):
                col = lambda n: cols[p][(n * 8 + s) * RW_HEAD_DIM:(n * 8 + s + 1) * RW_HEAD_DIM, :]
                st = states[p]
                sa = jnp.sum(st * col(0), axis=0, keepdims=True)
                st = st * col(1) - col(2) * sa + col(3) * v8[p][s:s + 1, :]
                ys[p].append(jnp.sum(st * col(4), axis=0, keepdims=True))
                states[p] = st
        for p in range(pairs):
            y_ref[rows, lanes[p]] = jnp.concatenate(ys[p], axis=0)
        return tuple(states)

    final = lax.fori_loop(0, tt // 8, step8, tuple(s_scr[p] for p in range(pairs)))
    for p in range(pairs):
        s_scr[p] = final[p]

    @pl.when(tb == n_tb - 1)
    def _():
        for p in range(pairs):
            st_ref[0, 2 * p] = final[p][:, :RW_HEAD_DIM]
            st_ref[0, 2 * p + 1] = final[p][:, RW_HEAD_DIM:]


def _rwkv_scan(r, w, k, v, kk, b, s0, *, n_seq, seq_len, tt, pairs):
    n_tb = seq_len // tt
    width = pairs * LANES
    heads = 2 * pairs
    t_spec = pl.BlockSpec((tt, width), lambda s, g, t: (s * n_tb + t, g))
    s_spec = pl.BlockSpec((1, heads, RW_HEAD_DIM, RW_HEAD_DIM), lambda s, g, t: (s, g, 0, 0))
    return pl.pallas_call(
        functools.partial(_rwkv_scan_kernel, tt=tt, pairs=pairs),
        grid=(n_seq, RW_WIDTH // width, n_tb),
        in_specs=[t_spec] * 6 + [s_spec],
        out_specs=[t_spec, s_spec],
        out_shape=[jax.ShapeDtypeStruct((n_seq * seq_len, RW_WIDTH), F32),
                   jax.ShapeDtypeStruct((n_seq, RW_HEADS, RW_HEAD_DIM, RW_HEAD_DIM), F32)],
        scratch_shapes=[pltpu.VMEM((pairs, RW_HEAD_DIM, LANES), F32)],
        compiler_params=_params(("parallel", "parallel", "arbitrary")),
        name="rwkv_scan",
    )(r, w, k, v, kk, b, s0)


def _rwkv_post_kernel(y_ref, r_ref, k_ref, v_ref, g_ref, lnw_ref, lnb_ref, rk_ref, o_ref):
    q = _half_ones().astype(F32)
    seg = lambda x: jnp.dot(x, q, precision=HIGHEST, preferred_element_type=F32)
    for c in range(RW_WIDTH // LANES):
        sl = slice(c * LANES, (c + 1) * LANES)
        y = y_ref[:, sl]
        mean = seg(y) * (1.0 / RW_HEAD_DIM)
        d = y - mean
        var = seg(d * d) * (1.0 / RW_HEAD_DIM)
        yn = d * lax.rsqrt(var + RW_LN_EPS) * lnw_ref[:, sl] + lnb_ref[:, sl]
        bonus = seg(r_ref[:, sl] * k_ref[:, sl] * rk_ref[:, sl]) * v_ref[:, sl]
        o_ref[:, sl] = ((yn + bonus) * g_ref[:, sl]).astype(o_ref.dtype)


def _rwkv_post(y, r, k, v, g, lp, tt):
    m = y.shape[0]
    row = lambda a: a.reshape(1, -1)
    t_spec = pl.BlockSpec((tt, RW_WIDTH), lambda i: (i, 0))
    p_spec = pl.BlockSpec((1, RW_WIDTH), lambda i: (0, 0))
    return pl.pallas_call(
        _rwkv_post_kernel,
        grid=(m // tt,),
        in_specs=[t_spec] * 5 + [p_spec] * 3,
        out_specs=t_spec,
        out_shape=jax.ShapeDtypeStruct((m, RW_WIDTH), BF16),
        compiler_params=_params(("parallel",)),
        name="rwkv_post",
    )(y, r, k, v, g, row(lp['rwkv_ln_w']), row(lp['rwkv_ln_b']), row(lp['rwkv_r_k']))


BAND_TQ = 512


def _softmax2(s1, s2):
    m = jnp.maximum(jnp.max(s1, axis=-1, keepdims=True), jnp.max(s2, axis=-1, keepdims=True))
    p1 = jnp.exp(s1 - m)
    p2 = jnp.exp(s2 - m)
    denom = jnp.sum(p1, axis=-1, keepdims=True) + jnp.sum(p2, axis=-1, keepdims=True)
    return p1, p2, denom


def _dot_nt(a, b):
    return lax.dot_general(a, b, (((1,), (1,)), ((), ())), preferred_element_type=F32)


def _band_prompt_kernel(q_ref, kp_ref, kc_ref, vp_ref, vc_ref, bias_ref, o_ref):
    i = pl.program_id(1)
    scale = ATT_HEAD_DIM ** -0.5
    q = q_ref[...].astype(BF16)
    s1 = _dot_nt(q, kp_ref[...].astype(BF16)) * scale + bias_ref[:, :BAND_TQ]
    s2 = _dot_nt(q, kc_ref[...].astype(BF16)) * scale + bias_ref[:, BAND_TQ:]
    s1 = jnp.where(i == 0, NEG_INF, s1)
    p1, p2, denom = _softmax2(s1, s2)
    o = (jnp.dot(p1.astype(BF16), vp_ref[...].astype(BF16), preferred_element_type=F32)
         + jnp.dot(p2.astype(BF16), vc_ref[...].astype(BF16), preferred_element_type=F32))
    o_ref[...] = (o / denom).astype(o_ref.dtype)


def _band_prompt(qkv, bias):
    nq = SEQ // BAND_TQ
    prev = lambda h, i: jnp.maximum(i - 1, 0)
    blk = (BAND_TQ, ATT_HEAD_DIM)
    return pl.pallas_call(
        _band_prompt_kernel,
        grid=(ATT_HEADS, nq),
        in_specs=[
            pl.BlockSpec(blk, lambda h, i: (i, h)),
            pl.BlockSpec(blk, lambda h, i: (prev(h, i), ATT_HEADS + h)),
            pl.BlockSpec(blk, lambda h, i: (i, ATT_HEADS + h)),
            pl.BlockSpec(blk, lambda h, i: (prev(h, i), 2 * ATT_HEADS + h)),
            pl.BlockSpec(blk, lambda h, i: (i, 2 * ATT_HEADS + h)),
            pl.BlockSpec((None, BAND_TQ, 2 * BAND_TQ), lambda h, i: (h, 0, 0)),
        ],
        out_specs=pl.BlockSpec(blk, lambda h, i: (i, h)),
        out_shape=jax.ShapeDtypeStruct((SEQ, ATT_WIDTH), BF16),
        compiler_params=_params(("parallel", "parallel")),
        name="band_attention_prompt",
    )(qkv, qkv, qkv, qkv, qkv, bias)


def _band_prompt_bias(rel_bias):
    iq = np.arange(BAND_TQ)[:, None]
    ik = np.arange(2 * BAND_TQ)[None, :]
    period = 3 * BAND_TQ
    m = np.arange(period)
    k_minus_q = np.where(m <= 2 * BAND_TQ, m, m - period)
    idx = np.clip(BAND_TQ - k_minus_q, -REL_CLIP, REL_CLIP) + REL_CLIP
    vec = jnp.take(rel_bias, jnp.asarray(idx), axis=1)
    flat = jnp.broadcast_to(vec[:, None, :], (ATT_HEADS, BAND_TQ, period)).reshape(ATT_HEADS, -1)
    table = flat[:, :BAND_TQ * (period - 1)].reshape(ATT_HEADS, BAND_TQ, period - 1)[:, :, :2 * BAND_TQ]
    qc = iq // CHUNK + N_PREV_CHUNKS
    kc = ik // CHUNK
    mask = (kc <= qc) & (kc >= qc - N_PREV_CHUNKS)
    return jnp.where(jnp.asarray(mask)[None], table, NEG_INF).astype(F32)


def _band_sample_kernel(q_ref, k_ref, v_ref, kc_ref, vc_ref, b1_ref, b2_ref, o_ref):
    scale = ATT_HEAD_DIM ** -0.5
    for h in range(ATT_HEADS):
        sl = slice(h * ATT_HEAD_DIM, (h + 1) * ATT_HEAD_DIM)
        q = q_ref[:, sl].astype(BF16)
        s1 = _dot_nt(q, kc_ref[:, sl].astype(BF16)) * scale + b1_ref[h]
        s2 = _dot_nt(q, k_ref[:, sl].astype(BF16)) * scale + b2_ref[h]
        p1, p2, denom = _softmax2(s1, s2)
        o = (jnp.dot(p1.astype(BF16), vc_ref[:, sl].astype(BF16), preferred_element_type=F32)
             + jnp.dot(p2.astype(BF16), v_ref[:, sl].astype(BF16), preferred_element_type=F32))
        o_ref[:, sl] = (o / denom).astype(o_ref.dtype)


def _band_sample(qkv, cache_k, cache_v, bias_past, bias_new):
    row0 = SEQ // DEC_SEQ
    new = lambda c: pl.BlockSpec((DEC_SEQ, ATT_WIDTH), lambda b: (row0 + b, c))
    cache = pl.BlockSpec((None, ATT_PAST, ATT_WIDTH), lambda b: (b, 0, 0))
    return pl.pallas_call(
        _band_sample_kernel,
        grid=(DEC_BATCH,),
        in_specs=[new(0), new(1), new(2), cache, cache,
                  pl.BlockSpec((ATT_HEADS, DEC_SEQ, ATT_PAST), lambda b: (0, 0, 0)),
                  pl.BlockSpec((ATT_HEADS, DEC_SEQ, DEC_SEQ), lambda b: (0, 0, 0))],
        out_specs=pl.BlockSpec((DEC_SEQ, ATT_WIDTH), lambda b: (b, 0)),
        out_shape=jax.ShapeDtypeStruct((DEC_BATCH * DEC_SEQ, ATT_WIDTH), BF16),
        compiler_params=_params(("parallel",)),
        name="band_attention_sample",
    )(qkv, qkv, qkv, cache_k, cache_v, bias_past, bias_new)


def _band_sample_bias(rel_bias):
    q_pos = PAST_LEN + np.arange(DEC_SEQ)[:, None]
    k_pos = PAST_LEN - ATT_PAST + np.arange(ATT_PAST + DEC_SEQ)[None, :]
    rel = np.clip(q_pos - k_pos, -REL_CLIP, REL_CLIP) + REL_CLIP
    qc, kc = q_pos // CHUNK, k_pos // CHUNK
    mask = (k_pos >= 0) & (kc <= qc) & (kc >= qc - N_PREV_CHUNKS)
    table = jnp.take(rel_bias, jnp.asarray(rel.reshape(-1)), axis=1).reshape(
        ATT_HEADS, DEC_SEQ, ATT_PAST + DEC_SEQ)
    table = jnp.where(jnp.asarray(mask)[None], table, NEG_INF).astype(F32)
    return table[:, :, :ATT_PAST], table[:, :, ATT_PAST:]


def _mem_attn_kernel(q_ref, k_ref, v_ref, o_ref):
    s = _dot_nt(q_ref[...], k_ref[...].astype(BF16)) * (MEM_HEAD_DIM ** -0.5)
    m = jnp.max(s, axis=-1, keepdims=True)
    p = jnp.exp(s - m)
    denom = jnp.sum(p, axis=-1, keepdims=True)
    o = jnp.dot(p.astype(BF16), v_ref[...].astype(BF16), preferred_element_type=F32)
    o_ref[...] = (o / denom).astype(o_ref.dtype)


def _mem_attn(q_mem, mem_k, mem_v, *, row0, n_rows, tq, rows_per_mem):
    off = row0 // tq
    kv_spec = pl.BlockSpec((None, N_MEM, MEM_HEAD_DIM), lambda h, i: ((i * tq) // rows_per_mem, 0, h))
    return pl.pallas_call(
        _mem_attn_kernel,
        grid=(MEM_HEADS, n_rows // tq),
        in_specs=[pl.BlockSpec((tq, MEM_HEAD_DIM), lambda h, i: (off + i, h)), kv_spec, kv_spec],
        out_specs=pl.BlockSpec((tq, MEM_HEAD_DIM), lambda h, i: (i, h)),
        out_shape=jax.ShapeDtypeStruct((n_rows, MEM_WIDTH), BF16),
        compiler_params=_params(("parallel", "parallel")),
        name="memory_attention",
    )(q_mem, mem_k, mem_v)


ROUTER_COLS = 128


def _norm_router_kernel(x_ref, g_ref, wr_ref, br_ref, h_ref, id_ref, wt_ref):
    x = x_ref[...]
    ms = jnp.mean(x * x, axis=-1, keepdims=True)
    h = x * lax.rsqrt(ms + NORM_EPS) * g_ref[...]
    h_ref[...] = h.astype(h_ref.dtype)
    logits = jnp.dot(h, wr_ref[...], precision=HIGHEST, preferred_element_type=F32) + br_ref[...]

    lane = lax.broadcasted_iota(jnp.int32, logits.shape, 1).astype(F32)
    row_max = lambda a: jnp.max(a, axis=-1, keepdims=True)
    first_at = lambda a, top: jnp.min(jnp.where(a == top, lane, float(ROUTER_COLS)), axis=-1, keepdims=True)
    is_group = lane < N_GROUPS
    g_logit = jnp.where(is_group, logits, NEG_INF)
    g_top = row_max(g_logit)
    g_idx = first_at(g_logit, g_top)
    p_group = 1.0 / jnp.sum(jnp.where(is_group, jnp.exp(logits - g_top), 0.0), axis=-1, keepdims=True)
    lo = N_GROUPS + g_idx * EXPERTS_PER_GROUP
    e_logit = jnp.where((lane >= lo) & (lane < lo + EXPERTS_PER_GROUP), logits, NEG_INF)
    e1 = row_max(e_logit)
    i1 = first_at(e_logit, e1)
    e_rest = jnp.where(lane == i1, NEG_INF, e_logit)
    e2 = row_max(e_rest)
    i2 = first_at(e_rest, e2)
    t = jnp.exp(e2 - e1)
    w1 = p_group / (1.0 + t)
    w2 = p_group * t / (1.0 + t)
    id_ref[...] = jnp.where(lane == 0, i1 - N_GROUPS, jnp.where(lane == 1, i2 - N_GROUPS, 0.0)).astype(jnp.int32)
    wt_ref[...] = jnp.where(lane == 0, w1, jnp.where(lane == 1, w2, 0.0))


def _norm_router(x, g, w_router, b_router, tm):
    m, d = x.shape
    return pl.pallas_call(
        _norm_router_kernel,
        grid=(m // tm,),
        in_specs=[pl.BlockSpec((tm, d), lambda i: (i, 0)),
                  pl.BlockSpec((1, d), lambda i: (0, 0)),
                  pl.BlockSpec((d, ROUTER_COLS), lambda i: (0, 0)),
                  pl.BlockSpec((1, ROUTER_COLS), lambda i: (0, 0))],
        out_specs=[pl.BlockSpec((tm, d), lambda i: (i, 0)),
                   pl.BlockSpec((tm, ROUTER_COLS), lambda i: (i, 0)),
                   pl.BlockSpec((tm, ROUTER_COLS), lambda i: (i, 0))],
        out_shape=[jax.ShapeDtypeStruct((m, d), BF16),
                   jax.ShapeDtypeStruct((m, ROUTER_COLS), jnp.int32),
                   jax.ShapeDtypeStruct((m, ROUTER_COLS), F32)],
        compiler_params=_params(("parallel",)),
        name="ffn_norm_router",
    )(x, g.reshape(1, d), w_router, b_router)


def _expert_kernel(blk_e_ref, blk_ok_ref, x_ref, wg_ref, wu_ref, wd_ref, rw_ref, o_ref):
    b = pl.program_id(0)
    f = pl.program_id(1)
    ok = blk_ok_ref[b] > 0

    @pl.when(ok)
    def _():
        x = x_ref[...]
        gate = jnp.dot(x, wg_ref[...].astype(BF16), preferred_element_type=F32)
        up = jnp.dot(x, wu_ref[...].astype(BF16), preferred_element_type=F32)
        hidden = (gate * _sigmoid(gate) * up).astype(BF16)
        part = jnp.dot(hidden, wd_ref[...].astype(BF16), preferred_element_type=F32)

        @pl.when(f == 0)
        def _():
            o_ref[...] = part

        @pl.when(f > 0)
        def _():
            o_ref[...] += part

        @pl.when(f == pl.num_programs(1) - 1)
        def _():
            o_ref[...] = o_ref[...] * rw_ref[...]

    @pl.when(jnp.logical_not(ok) & (f == 0))
    def _():
        o_ref[...] = jnp.zeros_like(o_ref)


def _expert_ffn(xs, row_w, blk_e, blk_ok, w_gate, w_up, w_down):
    rows = xs.shape[0]
    nb = rows // MOE_ROWS
    nf = D_EXPERT // MOE_FCHUNK
    fidx = lambda b, f, ok: jnp.where(ok[b] > 0, f, nf - 1)
    grid_spec = pltpu.PrefetchScalarGridSpec(
        num_scalar_prefetch=2,
        grid=(nb, nf),
        in_specs=[
            pl.BlockSpec((MOE_ROWS, D_MODEL), lambda b, f, e, ok: (b, 0)),
            pl.BlockSpec((None, D_MODEL, MOE_FCHUNK), lambda b, f, e, ok: (e[b], 0, fidx(b, f, ok))),
            pl.BlockSpec((None, D_MODEL, MOE_FCHUNK), lambda b, f, e, ok: (e[b], 0, fidx(b, f, ok))),
            pl.BlockSpec((None, MOE_FCHUNK, D_MODEL), lambda b, f, e, ok: (e[b], fidx(b, f, ok), 0)),
            pl.BlockSpec((MOE_ROWS, 1), lambda b, f, e, ok: (b, 0)),
        ],
        out_specs=pl.BlockSpec((MOE_ROWS, D_MODEL), lambda b, f, e, ok: (b, 0)),
    )
    return pl.pallas_call(
        _expert_kernel,
        grid_spec=grid_spec,
        out_shape=jax.ShapeDtypeStruct((rows, D_MODEL), F32),
        compiler_params=_params(("arbitrary", "arbitrary"), MOE_VMEM_LIMIT),
        name="expert_ffn",
    )(blk_e, blk_ok, xs, w_gate, w_up, w_down, row_w)


def _moe_layout(expert_id, weights):
    n = expert_id.shape[0]
    a = n * TOP_K
    n_blocks = a // MOE_ROWS + N_EXPERTS
    rows = n_blocks * MOE_ROWS
    flat_e = expert_id.reshape(a)
    flat_tok = jnp.arange(a, dtype=jnp.int32) // TOP_K
    order = jnp.argsort(flat_e)
    sorted_e = flat_e[order]
    counts = jnp.bincount(flat_e, length=N_EXPERTS)
    start = jnp.cumsum(counts) - counts
    padded = (counts + MOE_ROWS - 1) // MOE_ROWS * MOE_ROWS
    pad_end = jnp.cumsum(padded)
    pad_start = pad_end - padded
    dest = (pad_start[sorted_e] + (jnp.arange(a, dtype=jnp.int32) - start[sorted_e])).astype(jnp.int32)
    row_tok = jnp.zeros((rows,), jnp.int32).at[dest].set(flat_tok[order])
    row_w = jnp.zeros((rows,), F32).at[dest].set(weights.reshape(a)[order])
    pos = jnp.zeros((a,), jnp.int32).at[order].set(dest).reshape(n, TOP_K)
    blk_start = jnp.arange(n_blocks, dtype=jnp.int32) * MOE_ROWS
    blk_ok = (blk_start < pad_end[-1]).astype(jnp.int32)
    blk_e = jnp.minimum(jnp.searchsorted(pad_end, blk_start, side='right'), N_EXPERTS - 1)
    last_e = jnp.max(jnp.where(counts > 0, jnp.arange(N_EXPERTS), 0))
    blk_e = jnp.where(blk_ok > 0, blk_e, last_e).astype(jnp.int32)
    return row_tok, row_w, pos, blk_e, blk_ok


def kernel(x_prompt, x_sample, mem_prompt, cache_att_k, cache_att_v, cache_mem_k, cache_mem_v, state_rwkv, state_rwkv_shift, norm_mix_g, w_in, rwkv_mu, rwkv_w0, rwkv_w2, rwkv_a0, rwkv_a2, rwkv_g2, rwkv_k_k, rwkv_k_a, rwkv_r_k, rwkv_ln_w, rwkv_ln_b, w_rwkv_out, att_q_g, att_k_g, att_rel_bias, w_att_out, mem_norm_g, w_mem_kv, mem_q_g, mem_k_g, w_mem_out, w_o, norm_ffn_g, w_router_group, b_router_group, w_router_expert, b_router_expert, w_exp_gate, w_exp_up, w_exp_down):
    lp = dict(rwkv_mu=rwkv_mu[0], rwkv_w0=rwkv_w0[0], rwkv_w2=rwkv_w2[0], rwkv_a0=rwkv_a0[0],
              rwkv_a2=rwkv_a2[0], rwkv_g2=rwkv_g2[0], rwkv_k_k=rwkv_k_k[0], rwkv_k_a=rwkv_k_a[0],
              rwkv_r_k=rwkv_r_k[0], rwkv_ln_w=rwkv_ln_w[0], rwkv_ln_b=rwkv_ln_b[0])
    n_samp = DEC_BATCH * DEC_SEQ
    x = jnp.concatenate([x_prompt.reshape(SEQ, D_MODEL), x_sample.reshape(n_samp, D_MODEL)], axis=0)

    h = _rmsnorm(x, norm_mix_g[0], 264, BF16)
    w_in0 = w_in[0]
    c1 = RW_COLS
    c2 = c1 + 3 * ATT_WIDTH
    c3 = c2 + MEM_WIDTH
    w_rw = jnp.pad(w_in0[:, :c1].astype(BF16), ((0, 0), (0, RW_COLS_PAD - RW_COLS)))
    z_rw = _matmul(_mm_plain_kernel, h, w_rw, (), (), tm=1056, tn=512, out_dtype=F32, name="proj_rwkv")

    qk_gain = jnp.concatenate([jnp.tile(att_q_g[0], ATT_HEADS), jnp.tile(att_k_g[0], ATT_HEADS),
                               jnp.ones((ATT_WIDTH,), F32)]).reshape(1, 3 * ATT_WIDTH)
    qkv = _matmul(functools.partial(_mm_headnorm_kernel, head=ATT_HEAD_DIM, n_norm_tiles=8),
                  h, w_in0[:, c1:c2].astype(BF16), (qk_gain,),
                  (pl.BlockSpec((1, 512), lambda j, i: (0, j)),),
                  tm=1056, tn=512, out_dtype=F32, name="proj_qkv")
    q_mem = _matmul(functools.partial(_mm_headnorm_kernel, head=MEM_HEAD_DIM, n_norm_tiles=MEM_HEADS),
                    h, w_in0[:, c2:c3].astype(BF16), (jnp.tile(mem_q_g[0], MEM_HEADS).reshape(1, MEM_WIDTH),),
                    (pl.BlockSpec((1, 512), lambda j, i: (0, j)),),
                    tm=1056, tn=512, out_dtype=BF16, name="proj_mem_q")
    gate = _matmul(_mm_sigmoid_kernel, h, w_in0[:, c3:].astype(BF16), (), (),
                   tm=1056, tn=512, out_dtype=BF16, name="proj_gate")

    shift_p = jnp.zeros((1, 1, RW_COLS), F32)
    prep_p = _rwkv_prep(z_rw, shift_p, lp, row0=0, n_seq=1, seq_len=SEQ, tt=128)
    prep_s = _rwkv_prep(z_rw, state_rwkv_shift[0], lp, row0=SEQ, n_seq=DEC_BATCH, seq_len=DEC_SEQ,
                        tt=DEC_SEQ)
    s0_p = jnp.zeros((1, RW_HEADS, RW_HEAD_DIM, RW_HEAD_DIM), F32)
    rw_parts = []
    states = []
    for (r, w, k, v, kk, b, g), s0, n_seq, seq_len, tt in (
            (prep_p, s0_p, 1, SEQ, 256), (prep_s, state_rwkv[0], DEC_BATCH, DEC_SEQ, DEC_SEQ)):
        y, s_new = _rwkv_scan(r, w, k, v, kk, b, jnp.swapaxes(s0, -1, -2),
                              n_seq=n_seq, seq_len=seq_len, tt=tt, pairs=4)
        rw_parts.append(_rwkv_post(y, r, k, v, g, lp, 256))
        states.append(jnp.swapaxes(s_new, -1, -2))
    a_rw = jnp.concatenate(rw_parts, axis=0)

    bias_p = _band_prompt_bias(att_rel_bias[0])
    bias_s1, bias_s2 = _band_sample_bias(att_rel_bias[0])
    att_p = _band_prompt(qkv, bias_p)
    att_s = _band_sample(qkv, cache_att_k[0].reshape(DEC_BATCH, ATT_PAST, ATT_WIDTH),
                         cache_att_v[0].reshape(DEC_BATCH, ATT_PAST, ATT_WIDTH), bias_s1, bias_s2)
    a_att = jnp.concatenate([att_p, att_s], axis=0)

    mem_h = _rmsnorm(mem_prompt.reshape(N_MEM, D_MODEL), mem_norm_g[0], N_MEM, BF16)
    w_kv = w_mem_kv[0]
    mk_p = _matmul(functools.partial(_mm_headnorm_kernel, head=MEM_HEAD_DIM, n_norm_tiles=MEM_HEADS),
                   mem_h, w_kv[:, :MEM_WIDTH].astype(BF16),
                   (jnp.tile(mem_k_g[0], MEM_HEADS).reshape(1, MEM_WIDTH),),
                   (pl.BlockSpec((1, 512), lambda j, i: (0, j)),),
                   tm=N_MEM, tn=512, out_dtype=F32, name="mem_k")
    mv_p = _matmul(_mm_plain_kernel, mem_h, w_kv[:, MEM_WIDTH:].astype(BF16), (), (),
                   tm=N_MEM, tn=512, out_dtype=F32, name="mem_v")
    mem_p = _mem_attn(q_mem, mk_p.reshape(1, N_MEM, MEM_WIDTH), mv_p.reshape(1, N_MEM, MEM_WIDTH),
                      row0=0, n_rows=SEQ, tq=512, rows_per_mem=SEQ)
    mem_s = _mem_attn(q_mem, cache_mem_k[0].reshape(DEC_BATCH, N_MEM, MEM_WIDTH),
                      cache_mem_v[0].reshape(DEC_BATCH, N_MEM, MEM_WIDTH),
                      row0=SEQ, n_rows=n_samp, tq=DEC_SEQ, rows_per_mem=DEC_SEQ)
    a_mem = jnp.concatenate([mem_p, mem_s], axis=0)

    merged = _gated_merge(a_rw, a_att, a_mem, w_rwkv_out[0].astype(BF16), w_att_out[0].astype(BF16),
                          w_mem_out[0].astype(BF16), gate, 528, 512)
    x1 = _matmul(_mm_residual_kernel, merged, w_o[0].astype(BF16), (x,),
                 (pl.BlockSpec((1056, 512), lambda j, i: (i, j)),),
                 tm=1056, tn=512, out_dtype=F32, name="proj_out")

    w_router = jnp.zeros((D_MODEL, ROUTER_COLS), F32)
    w_router = w_router.at[:, :N_GROUPS].set(w_router_group[0])
    w_router = w_router.at[:, N_GROUPS:N_GROUPS + N_EXPERTS].set(w_router_expert[0])
    b_router = jnp.zeros((1, ROUTER_COLS), F32)
    b_router = b_router.at[0, :N_GROUPS].set(b_router_group[0])
    b_router = b_router.at[0, N_GROUPS:N_GROUPS + N_EXPERTS].set(b_router_expert[0])
    h2, route_id, route_w = _norm_router(x1, norm_ffn_g[0], w_router, b_router, 264)
    row_tok, row_w, pos, blk_e, blk_ok = _moe_layout(route_id[:, :TOP_K], route_w[:, :TOP_K])
    take_rows = lambda a, idx: jnp.take(a, idx, axis=0, mode="clip")
    yb = _expert_ffn(take_rows(h2, row_tok), row_w.reshape(-1, 1), blk_e, blk_ok,
                     w_exp_gate[0], w_exp_up[0], w_exp_down[0])
    x2 = x1 + (take_rows(yb, pos[:, 0]) + take_rows(yb, pos[:, 1]))

    heads = lambda t, bsz, rows: t.reshape(1, bsz, rows, ATT_HEADS, ATT_HEAD_DIM)
    k_cols = slice(ATT_WIDTH, 2 * ATT_WIDTH)
    v_cols = slice(2 * ATT_WIDTH, 3 * ATT_WIDTH)
    keep = SEQ - N_PREV_CHUNKS * CHUNK
    z_s = z_rw[SEQ:, :RW_COLS].reshape(DEC_BATCH, DEC_SEQ, RW_COLS)
    return (
        x2[:SEQ].reshape(1, SEQ, D_MODEL),
        x2[SEQ:].reshape(DEC_BATCH, DEC_SEQ, D_MODEL),
        heads(qkv[keep:SEQ, k_cols], 1, SEQ - keep),
        heads(qkv[keep:SEQ, v_cols], 1, SEQ - keep),
        states[0].reshape(1, 1, RW_HEADS, RW_HEAD_DIM, RW_HEAD_DIM),
        z_rw[SEQ - 1:SEQ, :RW_COLS].reshape(1, 1, 1, RW_COLS),
        mk_p.reshape(1, 1, N_MEM, MEM_HEADS, MEM_HEAD_DIM),
        mv_p.reshape(1, 1, N_MEM, MEM_HEADS, MEM_HEAD_DIM),
        heads(qkv[SEQ:, k_cols], DEC_BATCH, DEC_SEQ),
        heads(qkv[SEQ:, v_cols], DEC_BATCH, DEC_SEQ),
        states[1].reshape(1, DEC_BATCH, RW_HEADS, RW_HEAD_DIM, RW_HEAD_DIM),
        z_s[:, -1:, :].reshape(1, DEC_BATCH, 1, RW_COLS),
    )
```

```python
import functools
import math

import numpy as np
import jax
import jax.numpy as jnp
from jax import lax
from jax.experimental import pallas as pl
from jax.experimental.pallas import tpu as pltpu

F32 = jnp.float32
BF16 = jnp.bfloat16
HIGHEST = lax.Precision.HIGHEST

D_MODEL = 4096
SEQ = 8192
DEC_BATCH = 16
DEC_SEQ = 16
PAST_LEN = 2048
N_TOK = SEQ + DEC_BATCH * DEC_SEQ
CHUNK = 64
N_PREV_CHUNKS = 8
RW_HEADS = 32
RW_HEAD_DIM = 64
RW_WIDTH = 2048
DECAY_LORA = 64
ICL_LORA = 64
GATE_LORA = 256
RW_LN_EPS = 64e-5
RW_COLS = 3 * RW_WIDTH + DECAY_LORA + ICL_LORA + GATE_LORA
RW_COLS_PAD = 6656
ATT_HEADS = 16
ATT_HEAD_DIM = 128
ATT_WIDTH = 2048
ATT_PAST = 512
REL_CLIP = 256
N_MEM = 256
MEM_HEADS = 4
MEM_HEAD_DIM = 512
MEM_WIDTH = 2048
N_GROUPS = 8
EXPERTS_PER_GROUP = 8
N_EXPERTS = 64
TOP_K = 2
D_EXPERT = 1024
NORM_EPS = 1e-6
NEG_INF = -1e30

LANES = 128
VMEM_LIMIT = 48 * 1024 * 1024
MOE_VMEM_LIMIT = 54 * 1024 * 1024
MOE_ROWS = 256
MOE_FCHUNK = 256


def _params(sem, limit=VMEM_LIMIT):
    return pltpu.CompilerParams(dimension_semantics=sem, vmem_limit_bytes=limit)


def _sigmoid(x):
    return 1.0 / (1.0 + jnp.exp(-x))


def _rmsnorm_kernel(x_ref, g_ref, o_ref):
    x = x_ref[...]
    ms = jnp.mean(x * x, axis=-1, keepdims=True)
    o_ref[...] = (x * lax.rsqrt(ms + NORM_EPS) * g_ref[...]).astype(o_ref.dtype)


def _rmsnorm(x, g, tm, out_dtype):
    m, d = x.shape
    return pl.pallas_call(
        _rmsnorm_kernel,
        grid=(m // tm,),
        in_specs=[pl.BlockSpec((tm, d), lambda i: (i, 0)),
                  pl.BlockSpec((1, d), lambda i: (0, 0))],
        out_specs=pl.BlockSpec((tm, d), lambda i: (i, 0)),
        out_shape=jax.ShapeDtypeStruct((m, d), out_dtype),
        compiler_params=_params(("parallel",)),
        name="rmsnorm",
    )(x, g.reshape(1, d))


def _mm_plain_kernel(x_ref, w_ref, o_ref):
    o_ref[...] = jnp.dot(x_ref[...], w_ref[...], preferred_element_type=F32).astype(o_ref.dtype)


def _mm_headnorm_kernel(x_ref, w_ref, g_ref, o_ref, *, head, n_norm_tiles):
    acc = jnp.dot(x_ref[...], w_ref[...], preferred_element_type=F32)
    j = pl.program_id(0)
    tn = acc.shape[1]

    @pl.when(j < n_norm_tiles)
    def _():
        for c in range(tn // head):
            blk = acc[:, c * head:(c + 1) * head]
            ms = jnp.mean(blk * blk, axis=-1, keepdims=True)
            o_ref[:, c * head:(c + 1) * head] = (
                blk * lax.rsqrt(ms + NORM_EPS) * g_ref[:, c * head:(c + 1) * head]).astype(o_ref.dtype)

    @pl.when(j >= n_norm_tiles)
    def _():
        o_ref[...] = acc.astype(o_ref.dtype)


def _mm_sigmoid_kernel(x_ref, w_ref, o_ref):
    acc = jnp.dot(x_ref[...], w_ref[...], preferred_element_type=F32)
    o_ref[...] = _sigmoid(acc).astype(o_ref.dtype)


def _mm_residual_kernel(x_ref, w_ref, r_ref, o_ref):
    o_ref[...] = r_ref[...] + jnp.dot(x_ref[...], w_ref[...], preferred_element_type=F32)


def _matmul(body, x, w, extras, extra_specs, *, tm, tn, out_dtype, name):
    m, k = x.shape
    n = w.shape[1]
    return pl.pallas_call(
        body,
        grid=(n // tn, m // tm),
        in_specs=[pl.BlockSpec((tm, k), lambda j, i: (i, 0)),
                  pl.BlockSpec((k, tn), lambda j, i: (0, j))] + list(extra_specs),
        out_specs=pl.BlockSpec((tm, tn), lambda j, i: (i, j)),
        out_shape=jax.ShapeDtypeStruct((m, n), out_dtype),
        compiler_params=_params(("parallel", "parallel")),
        name=name,
    )(x, w, *extras)


def _merge_kernel(a0_ref, a1_ref, a2_ref, w0_ref, w1_ref, w2_ref, g0_ref, g1_ref, g2_ref, o_ref):
    y0 = jnp.dot(a0_ref[...], w0_ref[...], preferred_element_type=F32)
    y1 = jnp.dot(a1_ref[...], w1_ref[...], preferred_element_type=F32)
    y2 = jnp.dot(a2_ref[...], w2_ref[...], preferred_element_type=F32)
    merged = (g0_ref[...].astype(F32) * y0 + g1_ref[...].astype(F32) * y1
              + g2_ref[...].astype(F32) * y2)
    o_ref[...] = merged.astype(o_ref.dtype)


def _gated_merge(a_rw, a_att, a_mem, w_rw, w_att, w_mem, gate, tm, tn):
    m, k = a_rw.shape
    n = w_rw.shape[1]
    nj = n // tn
    a_spec = pl.BlockSpec((tm, k), lambda j, i: (i, 0))
    w_spec = pl.BlockSpec((k, tn), lambda j, i: (0, j))
    g_specs = [pl.BlockSpec((tm, tn), functools.partial(lambda j, i, b: (i, b * nj + j), b=b))
               for b in range(3)]
    return pl.pallas_call(
        _merge_kernel,
        grid=(nj, m // tm),
        in_specs=[a_spec, a_spec, a_spec, w_spec, w_spec, w_spec] + g_specs,
        out_specs=pl.BlockSpec((tm, tn), lambda j, i: (i, j)),
        out_shape=jax.ShapeDtypeStruct((m, n), BF16),
        compiler_params=_params(("parallel", "parallel")),
        name="gated_merge",
    )(a_rw, a_att, a_mem, w_rw, w_att, w_mem, gate, gate, gate)


def _half_ones():
    qi = lax.broadcasted_iota(jnp.int32, (LANES, LANES), 0) // RW_HEAD_DIM
    qj = lax.broadcasted_iota(jnp.int32, (LANES, LANES), 1) // RW_HEAD_DIM
    return qi == qj


def _rwkv_prep_kernel(z_ref, zp_ref, sh_ref, mu_ref, w0_ref, a0_ref, kk_ref, ka_ref,
                      w2_ref, a2_ref, g2_ref,
                      r_o, w_o, k_o, v_o, kk_o, b_o, g_o, *, tiles_per_seq):
    i = pl.program_id(0)
    first = (i % tiles_per_seq) == 0
    z = z_ref[...]
    prev_row = jnp.where(first, sh_ref[...], zp_ref[7:8, :])
    rows = lax.broadcasted_iota(jnp.int32, z.shape, 0)
    z_prev = jnp.where(rows == 0, prev_row, pltpu.roll(z, 1, 0))
    zs = z + (z_prev - z) * mu_ref[...]

    c1, c2, c3 = RW_WIDTH, 2 * RW_WIDTH, 3 * RW_WIDTH
    c4, c5 = c3 + DECAY_LORA, c3 + DECAY_LORA + ICL_LORA
    r = zs[:, :c1]
    k = zs[:, c1:c2]
    v = zs[:, c2:c3]
    wl = w0_ref[...] + jnp.dot(jnp.tanh(zs[:, c3:c4]), w2_ref[...], precision=HIGHEST,
                               preferred_element_type=F32)
    decay = jnp.exp(-math.exp(-0.5) * _sigmoid(wl))
    a = _sigmoid(a0_ref[...] + jnp.dot(zs[:, c4:c5], a2_ref[...], precision=HIGHEST,
                                       preferred_element_type=F32))
    g = jnp.dot(_sigmoid(zs[:, c5:]), g2_ref[...], precision=HIGHEST, preferred_element_type=F32)

    q = _half_ones().astype(F32)
    kk = k * kk_ref[...]
    for c in range(RW_WIDTH // LANES):
        sl = slice(c * LANES, (c + 1) * LANES)
        blk = kk[:, sl]
        ss = jnp.dot(blk * blk, q, precision=HIGHEST, preferred_element_type=F32)
        blk = blk / jnp.maximum(jnp.sqrt(ss), 1e-12)
        kk_o[:, sl] = blk
        b_o[:, sl] = blk * a[:, sl]
    r_o[...] = r
    w_o[...] = decay
    k_o[...] = k * (1.0 + (a - 1.0) * ka_ref[...])
    v_o[...] = v
    g_o[...] = g


def _rwkv_prep(z_rw, shift, lp, *, row0, n_seq, seq_len, tt):
    tiles_per_seq = seq_len // tt
    n_tiles = n_seq * tiles_per_seq
    off = row0 // tt
    off8 = row0 // 8
    row = lambda a: a.reshape(1, -1)
    full = lambda shape: pl.BlockSpec(shape, lambda i: (0, 0))
    out_sd = jax.ShapeDtypeStruct((n_seq * seq_len, RW_WIDTH), F32)
    o_spec = pl.BlockSpec((tt, RW_WIDTH), lambda i: (i, 0))
    return pl.pallas_call(
        functools.partial(_rwkv_prep_kernel, tiles_per_seq=tiles_per_seq),
        grid=(n_tiles,),
        in_specs=[
            pl.BlockSpec((tt, RW_COLS), lambda i: (off + i, 0)),
            pl.BlockSpec((8, RW_COLS), lambda i: (jnp.maximum(off8 + i * (tt // 8) - 1, 0), 0)),
            pl.BlockSpec((None, 1, RW_COLS), lambda i: (i // tiles_per_seq, 0, 0)),
            full((1, RW_COLS)), full((1, RW_WIDTH)), full((1, RW_WIDTH)), full((1, RW_WIDTH)),
            full((1, RW_WIDTH)), full((DECAY_LORA, RW_WIDTH)), full((ICL_LORA, RW_WIDTH)),
            full((GATE_LORA, RW_WIDTH)),
        ],
        out_specs=[o_spec] * 7,
        out_shape=[out_sd] * 7,
        compiler_params=_params(("parallel",)),
        name="rwkv_prep",
    )(z_rw, z_rw, shift, row(lp['rwkv_mu']), row(lp['rwkv_w0']), row(lp['rwkv_a0']),
      row(lp['rwkv_k_k']), row(lp['rwkv_k_a']), lp['rwkv_w2'], lp['rwkv_a2'], lp['rwkv_g2'])


RW_CHUNK = 8


def _split_bf16(x):
    hi = x.astype(BF16)
    return hi, (x - hi.astype(F32)).astype(BF16)


def _rwkv_scan_kernel(r_ref, w_ref, k_ref, v_ref, kk_ref, b_ref, s0_ref, y_ref, st_ref, s_scr,
                      *, tt, pairs):
    c = RW_CHUNK
    tb = pl.program_id(1)
    n_tb = pl.num_programs(1)
    left64 = lax.broadcasted_iota(jnp.int32, (RW_HEAD_DIM, LANES), 1) < RW_HEAD_DIM
    row_c = lax.broadcasted_iota(jnp.int32, (c, LANES), 0)
    q = _half_ones().astype(BF16)
    q2 = jnp.concatenate([q, q], axis=0)
    row_2c = lax.broadcasted_iota(jnp.int32, (2 * c, LANES), 0)
    ones_rows = jnp.where(row_2c < 2, 1.0, 0.0).astype(BF16)

    def segsum(x):
        hi, lo = _split_bf16(x)
        return jnp.dot(jnp.concatenate([hi, lo], axis=1), q2, preferred_element_type=F32)

    def shift_rows(x, n, fill):
        return jnp.where(row_c >= n, pltpu.roll(x, n, 0), fill)

    def per_head_blocks(x):
        zero = jnp.zeros_like(x)
        return jnp.concatenate([jnp.where(left64, x, zero), jnp.where(left64, zero, x)], axis=0)

    def own_head(x):
        return jnp.where(left64, x[:RW_HEAD_DIM], x[RW_HEAD_DIM:])

    @pl.when(tb == 0)
    def _():
        for p in range(pairs):
            s_scr[p] = jnp.concatenate([s0_ref[0, 2 * p], s0_ref[0, 2 * p + 1]], axis=1)

    zero_2c = jnp.zeros((2 * c, LANES), BF16)

    def chunk(ci, carry):
        rows = pl.ds(pl.multiple_of(ci * c, c), c)
        lanes = [pl.ds(p * LANES, LANES) for p in range(pairs)]
        vs, keys, states, grams, from_state = [], [], [], [], []
        for p in range(pairs):
            r, w, k, v, kk, b = (ref[rows, lanes[p]] for ref in (r_ref, w_ref, k_ref, v_ref, kk_ref, b_ref))
            gam = w
            n = 1
            while n < c:
                gam = gam * shift_rows(gam, n, 1.0)
                n *= 2
            inv = 1.0 / gam
            a_t = -(shift_rows(gam, 1, 1.0) * kk)
            b_t = b * inv
            k_t = k * inv
            r_t = gam * r
            g_end = gam[c - 1:c, :]
            prods = []
            for s in range(c):
                bs, ks = b_t[s:s + 1, :], k_t[s:s + 1, :]
                prods += [a_t * bs, a_t * ks, r_t * bs, r_t * ks]
            grams.append(segsum(jnp.concatenate(prods, axis=0)))
            vs.append(v)
            keys.append((jnp.concatenate([b_t, k_t], axis=0) * g_end, g_end,
                         _split_bf16(jnp.concatenate([a_t, r_t], axis=0))))
        for p in range(pairs):
            st = s_scr[p]
            s_hi, s_lo = _split_bf16(st)
            ar_hi, ar_lo = keys[p][2]
            states.append(st)
            from_state.append(jnp.dot(
                jnp.concatenate([ar_hi, ar_hi, ar_lo], axis=1),
                jnp.concatenate([per_head_blocks(s_hi), per_head_blocks(s_lo), per_head_blocks(s_hi)], axis=0),
                preferred_element_type=F32))
        operands = []
        for p in range(pairs):
            v = vs[p]
            g = lambda s, m: grams[p][(4 * s + m) * c:(4 * s + m + 1) * c, :]
            d = from_state[p][:c]
            y = from_state[p][c:]
            for s in range(c - 1):
                d = d + jnp.where(row_c > s, g(s, 1), 0.0) * v[s:s + 1, :]
            sa_rows = []
            for s in range(c):
                sa = d[s:s + 1, :]
                sa_rows.append(sa)
                if s < c - 1:
                    d = d + g(s, 0) * sa
                y = (y + jnp.where(row_c >= s, g(s, 2), 0.0) * sa
                     + jnp.where(row_c >= s, g(s, 3), 0.0) * v[s:s + 1, :])
            y_ref[rows, lanes[p]] = y
            key, g_end, _ = keys[p]
            key_hi, key_lo = _split_bf16(key)
            val_hi, val_lo = _split_bf16(jnp.concatenate(sa_rows + [v], axis=0))
            g_hi = g_end.astype(BF16).astype(F32)
            g_rows = jnp.where(row_2c == 0, g_hi, jnp.where(row_2c == 1, g_end - g_hi, 0.0)).astype(BF16)
            operands.append((
                jnp.concatenate([key_hi, key_hi, key_lo, g_rows], axis=0),
                jnp.concatenate([
                    jnp.concatenate([val_hi, val_lo, val_hi, zero_2c], axis=0),
                    jnp.concatenate([zero_2c, zero_2c, zero_2c, ones_rows], axis=0)], axis=1)))
        for p in range(pairs):
            out = lax.dot_general(operands[p][0], operands[p][1], (((0,), (0,)), ((), ())),
                                  preferred_element_type=F32)
            s_scr[p] = own_head(out[:, LANES:]) * states[p] + own_head(out[:, :LANES])
        return carry

    lax.fori_loop(0, tt // c, chunk, 0)

    @pl.when(tb == n_tb - 1)
    def _():
        for p in range(pairs):
            st_ref[0, 2 * p] = s_scr[p][:, :RW_HEAD_DIM]
            st_ref[0, 2 * p + 1] = s_scr[p][:, RW_HEAD_DIM:]


def _rwkv_scan(r, w, k, v, kk, b, s0, *, n_seq, seq_len, tt):
    n_tb = seq_len // tt
    pairs = RW_HEADS // 2
    t_spec = pl.BlockSpec((tt, RW_WIDTH), lambda s, t: (s * n_tb + t, 0))
    s_spec = pl.BlockSpec((1, RW_HEADS, RW_HEAD_DIM, RW_HEAD_DIM), lambda s, t: (s, 0, 0, 0))
    return pl.pallas_call(
        functools.partial(_rwkv_scan_kernel, tt=tt, pairs=pairs),
        grid=(n_seq, n_tb),
        in_specs=[t_spec] * 6 + [s_spec],
        out_specs=[t_spec, s_spec],
        out_shape=[jax.ShapeDtypeStruct((n_seq * seq_len, RW_WIDTH), F32),
                   jax.ShapeDtypeStruct((n_seq, RW_HEADS, RW_HEAD_DIM, RW_HEAD_DIM), F32)],
        scratch_shapes=[pltpu.VMEM((pairs, RW_HEAD_DIM, LANES), F32)],
        compiler_params=_params(("parallel", "arbitrary")),
        name="rwkv_scan",
    )(r, w, k, v, kk, b, s0)


def _rwkv_post_kernel(y_ref, r_ref, k_ref, v_ref, g_ref, lnw_ref, lnb_ref, rk_ref, o_ref):
    q = _half_ones().astype(F32)
    seg = lambda x: jnp.dot(x, q, precision=HIGHEST, preferred_element_type=F32)
    for c in range(RW_WIDTH // LANES):
        sl = slice(c * LANES, (c + 1) * LANES)
        y = y_ref[:, sl]
        mean = seg(y) * (1.0 / RW_HEAD_DIM)
        d = y - mean
        var = seg(d * d) * (1.0 / RW_HEAD_DIM)
        yn = d * lax.rsqrt(var + RW_LN_EPS) * lnw_ref[:, sl] + lnb_ref[:, sl]
        bonus = seg(r_ref[:, sl] * k_ref[:, sl] * rk_ref[:, sl]) * v_ref[:, sl]
        o_ref[:, sl] = ((yn + bonus) * g_ref[:, sl]).astype(o_ref.dtype)


def _rwkv_post(y, r, k, v, g, lp, tt):
    m = y.shape[0]
    row = lambda a: a.reshape(1, -1)
    t_spec = pl.BlockSpec((tt, RW_WIDTH), lambda i: (i, 0))
    p_spec = pl.BlockSpec((1, RW_WIDTH), lambda i: (0, 0))
    return pl.pallas_call(
        _rwkv_post_kernel,
        grid=(m // tt,),
        in_specs=[t_spec] * 5 + [p_spec] * 3,
        out_specs=t_spec,
        out_shape=jax.ShapeDtypeStruct((m, RW_WIDTH), BF16),
        compiler_params=_params(("parallel",)),
        name="rwkv_post",
    )(y, r, k, v, g, row(lp['rwkv_ln_w']), row(lp['rwkv_ln_b']), row(lp['rwkv_r_k']))


BAND_TQ = 512


def _softmax2(s1, s2):
    m = jnp.maximum(jnp.max(s1, axis=-1, keepdims=True), jnp.max(s2, axis=-1, keepdims=True))
    p1 = jnp.exp(s1 - m)
    p2 = jnp.exp(s2 - m)
    denom = jnp.sum(p1, axis=-1, keepdims=True) + jnp.sum(p2, axis=-1, keepdims=True)
    return p1, p2, denom


def _dot_nt(a, b):
    return lax.dot_general(a, b, (((1,), (1,)), ((), ())), preferred_element_type=F32)


def _band_prompt_kernel(q_ref, kp_ref, kc_ref, vp_ref, vc_ref, bias_ref, o_ref):
    i = pl.program_id(1)
    scale = ATT_HEAD_DIM ** -0.5
    q = q_ref[...].astype(BF16)
    s1 = _dot_nt(q, kp_ref[...].astype(BF16)) * scale + bias_ref[:, :BAND_TQ]
    s2 = _dot_nt(q, kc_ref[...].astype(BF16)) * scale + bias_ref[:, BAND_TQ:]
    s1 = jnp.where(i == 0, NEG_INF, s1)
    p1, p2, denom = _softmax2(s1, s2)
    o = (jnp.dot(p1.astype(BF16), vp_ref[...].astype(BF16), preferred_element_type=F32)
         + jnp.dot(p2.astype(BF16), vc_ref[...].astype(BF16), preferred_element_type=F32))
    o_ref[...] = (o / denom).astype(o_ref.dtype)


def _band_prompt(qkv, bias):
    nq = SEQ // BAND_TQ
    prev = lambda h, i: jnp.maximum(i - 1, 0)
    blk = (BAND_TQ, ATT_HEAD_DIM)
    return pl.pallas_call(
        _band_prompt_kernel,
        grid=(ATT_HEADS, nq),
        in_specs=[
            pl.BlockSpec(blk, lambda h, i: (i, h)),
            pl.BlockSpec(blk, lambda h, i: (prev(h, i), ATT_HEADS + h)),
            pl.BlockSpec(blk, lambda h, i: (i, ATT_HEADS + h)),
            pl.BlockSpec(blk, lambda h, i: (prev(h, i), 2 * ATT_HEADS + h)),
            pl.BlockSpec(blk, lambda h, i: (i, 2 * ATT_HEADS + h)),
            pl.BlockSpec((None, BAND_TQ, 2 * BAND_TQ), lambda h, i: (h, 0, 0)),
        ],
        out_specs=pl.BlockSpec(blk, lambda h, i: (i, h)),
        out_shape=jax.ShapeDtypeStruct((SEQ, ATT_WIDTH), BF16),
        compiler_params=_params(("parallel", "parallel")),
        name="band_attention_prompt",
    )(qkv, qkv, qkv, qkv, qkv, bias)


def _band_prompt_bias(rel_bias):
    iq = np.arange(BAND_TQ)[:, None]
    ik = np.arange(2 * BAND_TQ)[None, :]
    period = 3 * BAND_TQ
    m = np.arange(period)
    k_minus_q = np.where(m <= 2 * BAND_TQ, m, m - period)
    idx = np.clip(BAND_TQ - k_minus_q, -REL_CLIP, REL_CLIP) + REL_CLIP
    vec = jnp.take(rel_bias, jnp.asarray(idx), axis=1)
    flat = jnp.broadcast_to(vec[:, None, :], (ATT_HEADS, BAND_TQ, period)).reshape(ATT_HEADS, -1)
    table = flat[:, :BAND_TQ * (period - 1)].reshape(ATT_HEADS, BAND_TQ, period - 1)[:, :, :2 * BAND_TQ]
    qc = iq // CHUNK + N_PREV_CHUNKS
    kc = ik // CHUNK
    mask = (kc <= qc) & (kc >= qc - N_PREV_CHUNKS)
    return jnp.where(jnp.asarray(mask)[None], table, NEG_INF).astype(F32)


def _band_sample_kernel(q_ref, k_ref, v_ref, kc_ref, vc_ref, b1_ref, b2_ref, o_ref):
    scale = ATT_HEAD_DIM ** -0.5
    for h in range(ATT_HEADS):
        sl = slice(h * ATT_HEAD_DIM, (h + 1) * ATT_HEAD_DIM)
        q = q_ref[:, sl].astype(BF16)
        s1 = _dot_nt(q, kc_ref[:, sl].astype(BF16)) * scale + b1_ref[h]
        s2 = _dot_nt(q, k_ref[:, sl].astype(BF16)) * scale + b2_ref[h]
        p1, p2, denom = _softmax2(s1, s2)
        o = (jnp.dot(p1.astype(BF16), vc_ref[:, sl].astype(BF16), preferred_element_type=F32)
             + jnp.dot(p2.astype(BF16), v_ref[:, sl].astype(BF16), preferred_element_type=F32))
        o_ref[:, sl] = (o / denom).astype(o_ref.dtype)


def _band_sample(qkv, cache_k, cache_v, bias_past, bias_new):
    row0 = SEQ // DEC_SEQ
    new = lambda c: pl.BlockSpec((DEC_SEQ, ATT_WIDTH), lambda b: (row0 + b, c))
    cache = pl.BlockSpec((None, ATT_PAST, ATT_WIDTH), lambda b: (b, 0, 0))
    return pl.pallas_call(
        _band_sample_kernel,
        grid=(DEC_BATCH,),
        in_specs=[new(0), new(1), new(2), cache, cache,
                  pl.BlockSpec((ATT_HEADS, DEC_SEQ, ATT_PAST), lambda b: (0, 0, 0)),
                  pl.BlockSpec((ATT_HEADS, DEC_SEQ, DEC_SEQ), lambda b: (0, 0, 0))],
        out_specs=pl.BlockSpec((DEC_SEQ, ATT_WIDTH), lambda b: (b, 0)),
        out_shape=jax.ShapeDtypeStruct((DEC_BATCH * DEC_SEQ, ATT_WIDTH), BF16),
        compiler_params=_params(("parallel",)),
        name="band_attention_sample",
    )(qkv, qkv, qkv, cache_k, cache_v, bias_past, bias_new)


def _band_sample_bias(rel_bias):
    q_pos = PAST_LEN + np.arange(DEC_SEQ)[:, None]
    k_pos = PAST_LEN - ATT_PAST + np.arange(ATT_PAST + DEC_SEQ)[None, :]
    rel = np.clip(q_pos - k_pos, -REL_CLIP, REL_CLIP) + REL_CLIP
    qc, kc = q_pos // CHUNK, k_pos // CHUNK
    mask = (k_pos >= 0) & (kc <= qc) & (kc >= qc - N_PREV_CHUNKS)
    table = jnp.take(rel_bias, jnp.asarray(rel.reshape(-1)), axis=1).reshape(
        ATT_HEADS, DEC_SEQ, ATT_PAST + DEC_SEQ)
    table = jnp.where(jnp.asarray(mask)[None], table, NEG_INF).astype(F32)
    return table[:, :, :ATT_PAST], table[:, :, ATT_PAST:]


def _mem_attn_kernel(q_ref, k_ref, v_ref, o_ref):
    s = _dot_nt(q_ref[...], k_ref[...].astype(BF16)) * (MEM_HEAD_DIM ** -0.5)
    m = jnp.max(s, axis=-1, keepdims=True)
    p = jnp.exp(s - m)
    denom = jnp.sum(p, axis=-1, keepdims=True)
    o = jnp.dot(p.astype(BF16), v_ref[...].astype(BF16), preferred_element_type=F32)
    o_ref[...] = (o / denom).astype(o_ref.dtype)


def _mem_attn(q_mem, mem_k, mem_v, *, row0, n_rows, tq, rows_per_mem):
    off = row0 // tq
    kv_spec = pl.BlockSpec((None, N_MEM, MEM_HEAD_DIM), lambda h, i: ((i * tq) // rows_per_mem, 0, h))
    return pl.pallas_call(
        _mem_attn_kernel,
        grid=(MEM_HEADS, n_rows // tq),
        in_specs=[pl.BlockSpec((tq, MEM_HEAD_DIM), lambda h, i: (off + i, h)), kv_spec, kv_spec],
        out_specs=pl.BlockSpec((tq, MEM_HEAD_DIM), lambda h, i: (i, h)),
        out_shape=jax.ShapeDtypeStruct((n_rows, MEM_WIDTH), BF16),
        compiler_params=_params(("parallel", "parallel")),
        name="memory_attention",
    )(q_mem, mem_k, mem_v)


ROUTER_COLS = 128


def _norm_router_kernel(x_ref, g_ref, wr_ref, br_ref, h_ref, id_ref, wt_ref):
    x = x_ref[...]
    ms = jnp.mean(x * x, axis=-1, keepdims=True)
    h = x * lax.rsqrt(ms + NORM_EPS) * g_ref[...]
    h_ref[...] = h.astype(h_ref.dtype)
    logits = jnp.dot(h, wr_ref[...], precision=HIGHEST, preferred_element_type=F32) + br_ref[...]

    lane = lax.broadcasted_iota(jnp.int32, logits.shape, 1).astype(F32)
    row_max = lambda a: jnp.max(a, axis=-1, keepdims=True)
    first_at = lambda a, top: jnp.min(jnp.where(a == top, lane, float(ROUTER_COLS)), axis=-1, keepdims=True)
    is_group = lane < N_GROUPS
    g_logit = jnp.where(is_group, logits, NEG_INF)
    g_top = row_max(g_logit)
    g_idx = first_at(g_logit, g_top)
    p_group = 1.0 / jnp.sum(jnp.where(is_group, jnp.exp(logits - g_top), 0.0), axis=-1, keepdims=True)
    lo = N_GROUPS + g_idx * EXPERTS_PER_GROUP
    e_logit = jnp.where((lane >= lo) & (lane < lo + EXPERTS_PER_GROUP), logits, NEG_INF)
    e1 = row_max(e_logit)
    i1 = first_at(e_logit, e1)
    e_rest = jnp.where(lane == i1, NEG_INF, e_logit)
    e2 = row_max(e_rest)
    i2 = first_at(e_rest, e2)
    t = jnp.exp(e2 - e1)
    w1 = p_group / (1.0 + t)
    w2 = p_group * t / (1.0 + t)
    id_ref[...] = jnp.where(lane == 0, i1 - N_GROUPS, jnp.where(lane == 1, i2 - N_GROUPS, 0.0)).astype(jnp.int32)
    wt_ref[...] = jnp.where(lane == 0, w1, jnp.where(lane == 1, w2, 0.0))


def _norm_router(x, g, w_router, b_router, tm):
    m, d = x.shape
    return pl.pallas_call(
        _norm_router_kernel,
        grid=(m // tm,),
        in_specs=[pl.BlockSpec((tm, d), lambda i: (i, 0)),
                  pl.BlockSpec((1, d), lambda i: (0, 0)),
                  pl.BlockSpec((d, ROUTER_COLS), lambda i: (0, 0)),
                  pl.BlockSpec((1, ROUTER_COLS), lambda i: (0, 0))],
        out_specs=[pl.BlockSpec((tm, d), lambda i: (i, 0)),
                   pl.BlockSpec((tm, ROUTER_COLS), lambda i: (i, 0)),
                   pl.BlockSpec((tm, ROUTER_COLS), lambda i: (i, 0))],
        out_shape=[jax.ShapeDtypeStruct((m, d), BF16),
                   jax.ShapeDtypeStruct((m, ROUTER_COLS), jnp.int32),
                   jax.ShapeDtypeStruct((m, ROUTER_COLS), F32)],
        compiler_params=_params(("parallel",)),
        name="ffn_norm_router",
    )(x, g.reshape(1, d), w_router, b_router)


def _expert_kernel(blk_e_ref, blk_ok_ref, x_ref, wg_ref, wu_ref, wd_ref, rw_ref, o_ref):
    b = pl.program_id(0)
    f = pl.program_id(1)
    ok = blk_ok_ref[b] > 0

    @pl.when(ok)
    def _():
        x = x_ref[...]
        gate = jnp.dot(x, wg_ref[...].astype(BF16), preferred_element_type=F32)
        up = jnp.dot(x, wu_ref[...].astype(BF16), preferred_element_type=F32)
        hidden = (gate * _sigmoid(gate) * up).astype(BF16)
        part = jnp.dot(hidden, wd_ref[...].astype(BF16), preferred_element_type=F32)

        @pl.when(f == 0)
        def _():
            o_ref[...] = part

        @pl.when(f > 0)
        def _():
            o_ref[...] += part

        @pl.when(f == pl.num_programs(1) - 1)
        def _():
            o_ref[...] = o_ref[...] * rw_ref[...]

    @pl.when(jnp.logical_not(ok) & (f == 0))
    def _():
        o_ref[...] = jnp.zeros_like(o_ref)


def _expert_ffn(xs, row_w, blk_e, blk_ok, w_gate, w_up, w_down):
    rows = xs.shape[0]
    nb = rows // MOE_ROWS
    nf = D_EXPERT // MOE_FCHUNK
    fidx = lambda b, f, ok: jnp.where(ok[b] > 0, f, nf - 1)
    grid_spec = pltpu.PrefetchScalarGridSpec(
        num_scalar_prefetch=2,
        grid=(nb, nf),
        in_specs=[
            pl.BlockSpec((MOE_ROWS, D_MODEL), lambda b, f, e, ok: (b, 0)),
            pl.BlockSpec((None, D_MODEL, MOE_FCHUNK), lambda b, f, e, ok: (e[b], 0, fidx(b, f, ok))),
            pl.BlockSpec((None, D_MODEL, MOE_FCHUNK), lambda b, f, e, ok: (e[b], 0, fidx(b, f, ok))),
            pl.BlockSpec((None, MOE_FCHUNK, D_MODEL), lambda b, f, e, ok: (e[b], fidx(b, f, ok), 0)),
            pl.BlockSpec((MOE_ROWS, 1), lambda b, f, e, ok: (b, 0)),
        ],
        out_specs=pl.BlockSpec((MOE_ROWS, D_MODEL), lambda b, f, e, ok: (b, 0)),
    )
    return pl.pallas_call(
        _expert_kernel,
        grid_spec=grid_spec,
        out_shape=jax.ShapeDtypeStruct((rows, D_MODEL), F32),
        compiler_params=_params(("arbitrary", "arbitrary"), MOE_VMEM_LIMIT),
        name="expert_ffn",
    )(blk_e, blk_ok, xs, w_gate, w_up, w_down, row_w)


def _moe_layout(expert_id, weights):
    n = expert_id.shape[0]
    a = n * TOP_K
    n_blocks = a // MOE_ROWS + N_EXPERTS
    rows = n_blocks * MOE_ROWS
    flat_e = expert_id.reshape(a)
    flat_tok = jnp.arange(a, dtype=jnp.int32) // TOP_K
    order = jnp.argsort(flat_e)
    sorted_e = flat_e[order]
    counts = jnp.bincount(flat_e, length=N_EXPERTS)
    start = jnp.cumsum(counts) - counts
    padded = (counts + MOE_ROWS - 1) // MOE_ROWS * MOE_ROWS
    pad_end = jnp.cumsum(padded)
    pad_start = pad_end - padded
    dest = (pad_start[sorted_e] + (jnp.arange(a, dtype=jnp.int32) - start[sorted_e])).astype(jnp.int32)
    row_tok = jnp.zeros((rows,), jnp.int32).at[dest].set(flat_tok[order])
    row_w = jnp.zeros((rows,), F32).at[dest].set(weights.reshape(a)[order])
    pos = jnp.zeros((a,), jnp.int32).at[order].set(dest).reshape(n, TOP_K)
    blk_start = jnp.arange(n_blocks, dtype=jnp.int32) * MOE_ROWS
    blk_ok = (blk_start < pad_end[-1]).astype(jnp.int32)
    blk_e = jnp.minimum(jnp.searchsorted(pad_end, blk_start, side='right'), N_EXPERTS - 1)
    last_e = jnp.max(jnp.where(counts > 0, jnp.arange(N_EXPERTS), 0))
    blk_e = jnp.where(blk_ok > 0, blk_e, last_e).astype(jnp.int32)
    return row_tok, row_w, pos, blk_e, blk_ok


def kernel(x_prompt, x_sample, mem_prompt, cache_att_k, cache_att_v, cache_mem_k, cache_mem_v, state_rwkv, state_rwkv_shift, norm_mix_g, w_in, rwkv_mu, rwkv_w0, rwkv_w2, rwkv_a0, rwkv_a2, rwkv_g2, rwkv_k_k, rwkv_k_a, rwkv_r_k, rwkv_ln_w, rwkv_ln_b, w_rwkv_out, att_q_g, att_k_g, att_rel_bias, w_att_out, mem_norm_g, w_mem_kv, mem_q_g, mem_k_g, w_mem_out, w_o, norm_ffn_g, w_router_group, b_router_group, w_router_expert, b_router_expert, w_exp_gate, w_exp_up, w_exp_down):
    lp = dict(rwkv_mu=rwkv_mu[0], rwkv_w0=rwkv_w0[0], rwkv_w2=rwkv_w2[0], rwkv_a0=rwkv_a0[0],
              rwkv_a2=rwkv_a2[0], rwkv_g2=rwkv_g2[0], rwkv_k_k=rwkv_k_k[0], rwkv_k_a=rwkv_k_a[0],
              rwkv_r_k=rwkv_r_k[0], rwkv_ln_w=rwkv_ln_w[0], rwkv_ln_b=rwkv_ln_b[0])
    n_samp = DEC_BATCH * DEC_SEQ
    x = jnp.concatenate([x_prompt.reshape(SEQ, D_MODEL), x_sample.reshape(n_samp, D_MODEL)], axis=0)

    h = _rmsnorm(x, norm_mix_g[0], 264, BF16)
    w_in0 = w_in[0]
    c1 = RW_COLS
    c2 = c1 + 3 * ATT_WIDTH
    c3 = c2 + MEM_WIDTH
    w_rw = jnp.pad(w_in0[:, :c1].astype(BF16), ((0, 0), (0, RW_COLS_PAD - RW_COLS)))
    z_rw = _matmul(_mm_plain_kernel, h, w_rw, (), (), tm=1056, tn=512, out_dtype=F32, name="proj_rwkv")

    qk_gain = jnp.concatenate([jnp.tile(att_q_g[0], ATT_HEADS), jnp.tile(att_k_g[0], ATT_HEADS),
                               jnp.ones((ATT_WIDTH,), F32)]).reshape(1, 3 * ATT_WIDTH)
    qkv = _matmul(functools.partial(_mm_headnorm_kernel, head=ATT_HEAD_DIM, n_norm_tiles=8),
                  h, w_in0[:, c1:c2].astype(BF16), (qk_gain,),
                  (pl.BlockSpec((1, 512), lambda j, i: (0, j)),),
                  tm=1056, tn=512, out_dtype=F32, name="proj_qkv")
    q_mem = _matmul(functools.partial(_mm_headnorm_kernel, head=MEM_HEAD_DIM, n_norm_tiles=MEM_HEADS),
                    h, w_in0[:, c2:c3].astype(BF16), (jnp.tile(mem_q_g[0], MEM_HEADS).reshape(1, MEM_WIDTH),),
                    (pl.BlockSpec((1, 512), lambda j, i: (0, j)),),
                    tm=1056, tn=512, out_dtype=BF16, name="proj_mem_q")
    gate = _matmul(_mm_sigmoid_kernel, h, w_in0[:, c3:].astype(BF16), (), (),
                   tm=1056, tn=512, out_dtype=BF16, name="proj_gate")

    shift_p = jnp.zeros((1, 1, RW_COLS), F32)
    prep_p = _rwkv_prep(z_rw, shift_p, lp, row0=0, n_seq=1, seq_len=SEQ, tt=128)
    prep_s = _rwkv_prep(z_rw, state_rwkv_shift[0], lp, row0=SEQ, n_seq=DEC_BATCH, seq_len=DEC_SEQ,
                        tt=DEC_SEQ)
    s0_p = jnp.zeros((1, RW_HEADS, RW_HEAD_DIM, RW_HEAD_DIM), F32)
    rw_parts = []
    states = []
    for (r, w, k, v, kk, b, g), s0, n_seq, seq_len, tt in (
            (prep_p, s0_p, 1, SEQ, 128), (prep_s, state_rwkv[0], DEC_BATCH, DEC_SEQ, DEC_SEQ)):
        y, s_new = _rwkv_scan(r, w, k, v, kk, b, jnp.swapaxes(s0, -1, -2),
                              n_seq=n_seq, seq_len=seq_len, tt=tt)
        rw_parts.append(_rwkv_post(y, r, k, v, g, lp, 256))
        states.append(jnp.swapaxes(s_new, -1, -2))
    a_rw = jnp.concatenate(rw_parts, axis=0)

    bias_p = _band_prompt_bias(att_rel_bias[0])
    bias_s1, bias_s2 = _band_sample_bias(att_rel_bias[0])
    att_p = _band_prompt(qkv, bias_p)
    att_s = _band_sample(qkv, cache_att_k[0].reshape(DEC_BATCH, ATT_PAST, ATT_WIDTH),
                         cache_att_v[0].reshape(DEC_BATCH, ATT_PAST, ATT_WIDTH), bias_s1, bias_s2)
    a_att = jnp.concatenate([att_p, att_s], axis=0)

    mem_h = _rmsnorm(mem_prompt.reshape(N_MEM, D_MODEL), mem_norm_g[0], N_MEM, BF16)
    w_kv = w_mem_kv[0]
    mk_p = _matmul(functools.partial(_mm_headnorm_kernel, head=MEM_HEAD_DIM, n_norm_tiles=MEM_HEADS),
                   mem_h, w_kv[:, :MEM_WIDTH].astype(BF16),
                   (jnp.tile(mem_k_g[0], MEM_HEADS).reshape(1, MEM_WIDTH),),
                   (pl.BlockSpec((1, 512), lambda j, i: (0, j)),),
                   tm=N_MEM, tn=512, out_dtype=F32, name="mem_k")
    mv_p = _matmul(_mm_plain_kernel, mem_h, w_kv[:, MEM_WIDTH:].astype(BF16), (), (),
                   tm=N_MEM, tn=512, out_dtype=F32, name="mem_v")
    mem_p = _mem_attn(q_mem, mk_p.reshape(1, N_MEM, MEM_WIDTH), mv_p.reshape(1, N_MEM, MEM_WIDTH),
                      row0=0, n_rows=SEQ, tq=512, rows_per_mem=SEQ)
    mem_s = _mem_attn(q_mem, cache_mem_k[0].reshape(DEC_BATCH, N_MEM, MEM_WIDTH),
                      cache_mem_v[0].reshape(DEC_BATCH, N_MEM, MEM_WIDTH),
                      row0=SEQ, n_rows=n_samp, tq=DEC_SEQ, rows_per_mem=DEC_SEQ)
    a_mem = jnp.concatenate([mem_p, mem_s], axis=0)

    merged = _gated_merge(a_rw, a_att, a_mem, w_rwkv_out[0].astype(BF16), w_att_out[0].astype(BF16),
                          w_mem_out[0].astype(BF16), gate, 528, 512)
    x1 = _matmul(_mm_residual_kernel, merged, w_o[0].astype(BF16), (x,),
                 (pl.BlockSpec((1056, 512), lambda j, i: (i, j)),),
                 tm=1056, tn=512, out_dtype=F32, name="proj_out")

    w_router = jnp.zeros((D_MODEL, ROUTER_COLS), F32)
    w_router = w_router.at[:, :N_GROUPS].set(w_router_group[0])
    w_router = w_router.at[:, N_GROUPS:N_GROUPS + N_EXPERTS].set(w_router_expert[0])
    b_router = jnp.zeros((1, ROUTER_COLS), F32)
    b_router = b_router.at[0, :N_GROUPS].set(b_router_group[0])
    b_router = b_router.at[0, N_GROUPS:N_GROUPS + N_EXPERTS].set(b_router_expert[0])
    h2, route_id, route_w = _norm_router(x1, norm_ffn_g[0], w_router, b_router, 264)
    row_tok, row_w, pos, blk_e, blk_ok = _moe_layout(route_id[:, :TOP_K], route_w[:, :TOP_K])
    take_rows = lambda a, idx: jnp.take(a, idx, axis=0, mode="clip")
    yb = _expert_ffn(take_rows(h2, row_tok), row_w.reshape(-1, 1), blk_e, blk_ok,
                     w_exp_gate[0], w_exp_up[0], w_exp_down[0])
    x2 = x1 + (take_rows(yb, pos[:, 0]) + take_rows(yb, pos[:, 1]))

    heads = lambda t, bsz, rows: t.reshape(1, bsz, rows, ATT_HEADS, ATT_HEAD_DIM)
    k_cols = slice(ATT_WIDTH, 2 * ATT_WIDTH)
    v_cols = slice(2 * ATT_WIDTH, 3 * ATT_WIDTH)
    keep = SEQ - N_PREV_CHUNKS * CHUNK
    z_s = z_rw[SEQ:, :RW_COLS].reshape(DEC_BATCH, DEC_SEQ, RW_COLS)
    return (
        x2[:SEQ].reshape(1, SEQ, D_MODEL),
        x2[SEQ:].reshape(DEC_BATCH, DEC_SEQ, D_MODEL),
        heads(qkv[keep:SEQ, k_cols], 1, SEQ - keep),
        heads(qkv[keep:SEQ, v_cols], 1, SEQ - keep),
        states[0].reshape(1, 1, RW_HEADS, RW_HEAD_DIM, RW_HEAD_DIM),
        z_rw[SEQ - 1:SEQ, :RW_COLS].reshape(1, 1, 1, RW_COLS),
        mk_p.reshape(1, 1, N_MEM, MEM_HEADS, MEM_HEAD_DIM),
        mv_p.reshape(1, 1, N_MEM, MEM_HEADS, MEM_HEAD_DIM),
        heads(qkv[SEQ:, k_cols], DEC_BATCH, DEC_SEQ),
        heads(qkv[SEQ:, v_cols], DEC_BATCH, DEC_SEQ),
        states[1].reshape(1, DEC_BATCH, RW_HEADS, RW_HEAD_DIM, RW_HEAD_DIM),
        z_s[:, -1:, :].reshape(1, DEC_BATCH, 1, RW_COLS),
    )
```

```python
import functools
import math

import numpy as np
import jax
import jax.numpy as jnp
from jax import lax
from jax.experimental import pallas as pl
from jax.experimental.pallas import tpu as pltpu

F32 = jnp.float32
BF16 = jnp.bfloat16
HIGHEST = lax.Precision.HIGHEST

D_MODEL = 4096
SEQ = 8192
DEC_BATCH = 16
DEC_SEQ = 16
PAST_LEN = 2048
N_TOK = SEQ + DEC_BATCH * DEC_SEQ
CHUNK = 64
N_PREV_CHUNKS = 8
RW_HEADS = 32
RW_HEAD_DIM = 64
RW_WIDTH = 2048
DECAY_LORA = 64
ICL_LORA = 64
GATE_LORA = 256
RW_LN_EPS = 64e-5
RW_COLS = 3 * RW_WIDTH + DECAY_LORA + ICL_LORA + GATE_LORA
RW_COLS_PAD = 6656
ATT_HEADS = 16
ATT_HEAD_DIM = 128
ATT_WIDTH = 2048
ATT_PAST = 512
REL_CLIP = 256
N_MEM = 256
MEM_HEADS = 4
MEM_HEAD_DIM = 512
MEM_WIDTH = 2048
N_GROUPS = 8
EXPERTS_PER_GROUP = 8
N_EXPERTS = 64
TOP_K = 2
D_EXPERT = 1024
NORM_EPS = 1e-6
NEG_INF = -1e30

LANES = 128
VMEM_LIMIT = 48 * 1024 * 1024
MOE_VMEM_LIMIT = 54 * 1024 * 1024
MOE_ROWS = 512
MOE_FCHUNK = 128
MOE_DOWN_COLS = 1024


def _params(sem, limit=VMEM_LIMIT):
    return pltpu.CompilerParams(dimension_semantics=sem, vmem_limit_bytes=limit)


def _sigmoid(x):
    return 1.0 / (1.0 + jnp.exp(-x))


def _rmsnorm_kernel(x_ref, g_ref, o_ref):
    x = x_ref[...]
    ms = jnp.mean(x * x, axis=-1, keepdims=True)
    o_ref[...] = (x * lax.rsqrt(ms + NORM_EPS) * g_ref[...]).astype(o_ref.dtype)


def _rmsnorm(x, g, tm, out_dtype):
    m, d = x.shape
    return pl.pallas_call(
        _rmsnorm_kernel,
        grid=(m // tm,),
        in_specs=[pl.BlockSpec((tm, d), lambda i: (i, 0)),
                  pl.BlockSpec((1, d), lambda i: (0, 0))],
        out_specs=pl.BlockSpec((tm, d), lambda i: (i, 0)),
        out_shape=jax.ShapeDtypeStruct((m, d), out_dtype),
        compiler_params=_params(("parallel",)),
        name="rmsnorm",
    )(x, g.reshape(1, d))


def _mm_plain_kernel(x_ref, w_ref, o_ref):
    o_ref[...] = jnp.dot(x_ref[...], w_ref[...], preferred_element_type=F32).astype(o_ref.dtype)


def _mm_headnorm_kernel(x_ref, w_ref, g_ref, o_ref, *, head, n_norm_tiles):
    acc = jnp.dot(x_ref[...], w_ref[...], preferred_element_type=F32)
    j = pl.program_id(0)
    tn = acc.shape[1]

    @pl.when(j < n_norm_tiles)
    def _():
        for c in range(tn // head):
            blk = acc[:, c * head:(c + 1) * head]
            ms = jnp.mean(blk * blk, axis=-1, keepdims=True)
            o_ref[:, c * head:(c + 1) * head] = (
                blk * lax.rsqrt(ms + NORM_EPS) * g_ref[:, c * head:(c + 1) * head]).astype(o_ref.dtype)

    @pl.when(j >= n_norm_tiles)
    def _():
        o_ref[...] = acc.astype(o_ref.dtype)


def _mm_sigmoid_kernel(x_ref, w_ref, o_ref):
    acc = jnp.dot(x_ref[...], w_ref[...], preferred_element_type=F32)
    o_ref[...] = _sigmoid(acc).astype(o_ref.dtype)


def _mm_residual_kernel(x_ref, w_ref, r_ref, o_ref):
    o_ref[...] = r_ref[...] + jnp.dot(x_ref[...], w_ref[...], preferred_element_type=F32)


def _matmul(body, x, w, extras, extra_specs, *, tm, tn, out_dtype, name):
    m, k = x.shape
    n = w.shape[1]
    return pl.pallas_call(
        body,
        grid=(n // tn, m // tm),
        in_specs=[pl.BlockSpec((tm, k), lambda j, i: (i, 0)),
                  pl.BlockSpec((k, tn), lambda j, i: (0, j))] + list(extra_specs),
        out_specs=pl.BlockSpec((tm, tn), lambda j, i: (i, j)),
        out_shape=jax.ShapeDtypeStruct((m, n), out_dtype),
        compiler_params=_params(("parallel", "parallel")),
        name=name,
    )(x, w, *extras)


def _merge_kernel(a0_ref, a1_ref, a2_ref, w0_ref, w1_ref, w2_ref, g0_ref, g1_ref, g2_ref, o_ref):
    y0 = jnp.dot(a0_ref[...], w0_ref[...], preferred_element_type=F32)
    y1 = jnp.dot(a1_ref[...], w1_ref[...], preferred_element_type=F32)
    y2 = jnp.dot(a2_ref[...], w2_ref[...], preferred_element_type=F32)
    merged = (g0_ref[...].astype(F32) * y0 + g1_ref[...].astype(F32) * y1
              + g2_ref[...].astype(F32) * y2)
    o_ref[...] = merged.astype(o_ref.dtype)


def _gated_merge(a_rw, a_att, a_mem, w_rw, w_att, w_mem, gate, tm, tn):
    m, k = a_rw.shape
    n = w_rw.shape[1]
    nj = n // tn
    a_spec = pl.BlockSpec((tm, k), lambda j, i: (i, 0))
    w_spec = pl.BlockSpec((k, tn), lambda j, i: (0, j))
    g_specs = [pl.BlockSpec((tm, tn), functools.partial(lambda j, i, b: (i, b * nj + j), b=b))
               for b in range(3)]
    return pl.pallas_call(
        _merge_kernel,
        grid=(nj, m // tm),
        in_specs=[a_spec, a_spec, a_spec, w_spec, w_spec, w_spec] + g_specs,
        out_specs=pl.BlockSpec((tm, tn), lambda j, i: (i, j)),
        out_shape=jax.ShapeDtypeStruct((m, n), BF16),
        compiler_params=_params(("parallel", "parallel")),
        name="gated_merge",
    )(a_rw, a_att, a_mem, w_rw, w_att, w_mem, gate, gate, gate)


def _half_ones():
    qi = lax.broadcasted_iota(jnp.int32, (LANES, LANES), 0) // RW_HEAD_DIM
    qj = lax.broadcasted_iota(jnp.int32, (LANES, LANES), 1) // RW_HEAD_DIM
    return qi == qj


def _rwkv_prep_kernel(z_ref, zp_ref, sh_ref, mu_ref, w0_ref, a0_ref, kk_ref, ka_ref,
                      w2_ref, a2_ref, g2_ref,
                      r_o, w_o, k_o, v_o, kk_o, b_o, g_o, *, tiles_per_seq):
    i = pl.program_id(0)
    first = (i % tiles_per_seq) == 0
    z = z_ref[...]
    prev_row = jnp.where(first, sh_ref[...], zp_ref[7:8, :])
    rows = lax.broadcasted_iota(jnp.int32, z.shape, 0)
    z_prev = jnp.where(rows == 0, prev_row, pltpu.roll(z, 1, 0))
    zs = z + (z_prev - z) * mu_ref[...]

    c1, c2, c3 = RW_WIDTH, 2 * RW_WIDTH, 3 * RW_WIDTH
    c4, c5 = c3 + DECAY_LORA, c3 + DECAY_LORA + ICL_LORA
    r = zs[:, :c1]
    k = zs[:, c1:c2]
    v = zs[:, c2:c3]
    wl = w0_ref[...] + jnp.dot(jnp.tanh(zs[:, c3:c4]), w2_ref[...], precision=HIGHEST,
                               preferred_element_type=F32)
    decay = jnp.exp(-math.exp(-0.5) * _sigmoid(wl))
    a = _sigmoid(a0_ref[...] + jnp.dot(zs[:, c4:c5], a2_ref[...], precision=HIGHEST,
                                       preferred_element_type=F32))
    g = jnp.dot(_sigmoid(zs[:, c5:]), g2_ref[...], precision=HIGHEST, preferred_element_type=F32)

    q = _half_ones().astype(F32)
    kk = k * kk_ref[...]
    for c in range(RW_WIDTH // LANES):
        sl = slice(c * LANES, (c + 1) * LANES)
        blk = kk[:, sl]
        ss = jnp.dot(blk * blk, q, precision=HIGHEST, preferred_element_type=F32)
        blk = blk / jnp.maximum(jnp.sqrt(ss), 1e-12)
        kk_o[:, sl] = blk
        b_o[:, sl] = blk * a[:, sl]
    r_o[...] = r
    w_o[...] = decay
    k_o[...] = k * (1.0 + (a - 1.0) * ka_ref[...])
    v_o[...] = v
    g_o[...] = g


def _rwkv_prep(z_rw, shift, lp, *, row0, n_seq, seq_len, tt):
    tiles_per_seq = seq_len // tt
    n_tiles = n_seq * tiles_per_seq
    off = row0 // tt
    off8 = row0 // 8
    row = lambda a: a.reshape(1, -1)
    full = lambda shape: pl.BlockSpec(shape, lambda i: (0, 0))
    out_sd = jax.ShapeDtypeStruct((n_seq * seq_len, RW_WIDTH), F32)
    o_spec = pl.BlockSpec((tt, RW_WIDTH), lambda i: (i, 0))
    return pl.pallas_call(
        functools.partial(_rwkv_prep_kernel, tiles_per_seq=tiles_per_seq),
        grid=(n_tiles,),
        in_specs=[
            pl.BlockSpec((tt, RW_COLS), lambda i: (off + i, 0)),
            pl.BlockSpec((8, RW_COLS), lambda i: (jnp.maximum(off8 + i * (tt // 8) - 1, 0), 0)),
            pl.BlockSpec((None, 1, RW_COLS), lambda i: (i // tiles_per_seq, 0, 0)),
            full((1, RW_COLS)), full((1, RW_WIDTH)), full((1, RW_WIDTH)), full((1, RW_WIDTH)),
            full((1, RW_WIDTH)), full((DECAY_LORA, RW_WIDTH)), full((ICL_LORA, RW_WIDTH)),
            full((GATE_LORA, RW_WIDTH)),
        ],
        out_specs=[o_spec] * 7,
        out_shape=[out_sd] * 7,
        compiler_params=_params(("parallel",)),
        name="rwkv_prep",
    )(z_rw, z_rw, shift, row(lp['rwkv_mu']), row(lp['rwkv_w0']), row(lp['rwkv_a0']),
      row(lp['rwkv_k_k']), row(lp['rwkv_k_a']), lp['rwkv_w2'], lp['rwkv_a2'], lp['rwkv_g2'])


RW_CHUNK = 8


def _split_bf16(x):
    hi = x.astype(BF16)
    return hi, (x - hi.astype(F32)).astype(BF16)


def _rwkv_scan_kernel(r_ref, w_ref, k_ref, v_ref, kk_ref, b_ref, s0_ref, y_ref, st_ref, s_scr,
                      *, tt, pairs):
    c = RW_CHUNK
    tb = pl.program_id(1)
    n_tb = pl.num_programs(1)
    left64 = lax.broadcasted_iota(jnp.int32, (RW_HEAD_DIM, LANES), 1) < RW_HEAD_DIM
    row_c = lax.broadcasted_iota(jnp.int32, (c, LANES), 0)
    q = _half_ones().astype(BF16)
    q2 = jnp.concatenate([q, q], axis=0)
    row_2c = lax.broadcasted_iota(jnp.int32, (2 * c, LANES), 0)
    ones_rows = jnp.where(row_2c < 2, 1.0, 0.0).astype(BF16)

    def segsum(x):
        hi, lo = _split_bf16(x)
        return jnp.dot(jnp.concatenate([hi, lo], axis=1), q2, preferred_element_type=F32)

    def shift_rows(x, n, fill):
        return jnp.where(row_c >= n, pltpu.roll(x, n, 0), fill)

    def per_head_blocks(x):
        zero = jnp.zeros_like(x)
        return jnp.concatenate([jnp.where(left64, x, zero), jnp.where(left64, zero, x)], axis=0)

    def own_head(x):
        return jnp.where(left64, x[:RW_HEAD_DIM], x[RW_HEAD_DIM:])

    @pl.when(tb == 0)
    def _():
        for p in range(pairs):
            s_scr[p] = jnp.concatenate([s0_ref[0, 2 * p], s0_ref[0, 2 * p + 1]], axis=1)

    zero_2c = jnp.zeros((2 * c, LANES), BF16)

    def chunk(ci, carry):
        rows = pl.ds(pl.multiple_of(ci * c, c), c)
        lanes = [pl.ds(p * LANES, LANES) for p in range(pairs)]
        vs, keys, states, grams, from_state = [], [], [], [], []
        for p in range(pairs):
            r, w, k, v, kk, b = (ref[rows, lanes[p]] for ref in (r_ref, w_ref, k_ref, v_ref, kk_ref, b_ref))
            gam = w
            n = 1
            while n < c:
                gam = gam * shift_rows(gam, n, 1.0)
                n *= 2
            inv = 1.0 / gam
            a_t = -(shift_rows(gam, 1, 1.0) * kk)
            b_t = b * inv
            k_t = k * inv
            r_t = gam * r
            g_end = gam[c - 1:c, :]
            prods = []
            for s in range(c):
                bs, ks = b_t[s:s + 1, :], k_t[s:s + 1, :]
                prods += [a_t * bs, a_t * ks, r_t * bs, r_t * ks]
            grams.append(segsum(jnp.concatenate(prods, axis=0)))
            vs.append(v)
            keys.append((jnp.concatenate([b_t, k_t], axis=0) * g_end, g_end,
                         _split_bf16(jnp.concatenate([a_t, r_t], axis=0))))
        for p in range(pairs):
            st = s_scr[p]
            s_hi, s_lo = _split_bf16(st)
            ar_hi, ar_lo = keys[p][2]
            states.append(st)
            from_state.append(jnp.dot(
                jnp.concatenate([ar_hi, ar_hi, ar_lo], axis=1),
                jnp.concatenate([per_head_blocks(s_hi), per_head_blocks(s_lo), per_head_blocks(s_hi)], axis=0),
                preferred_element_type=F32))
        operands = []
        for p in range(pairs):
            v = vs[p]
            g = lambda s, m: grams[p][(4 * s + m) * c:(4 * s + m + 1) * c, :]
            d = from_state[p][:c]
            y = from_state[p][c:]
            for s in range(c - 1):
                d = d + jnp.where(row_c > s, g(s, 1), 0.0) * v[s:s + 1, :]
            sa_rows = []
            for s in range(c):
                sa = d[s:s + 1, :]
                sa_rows.append(sa)
                if s < c - 1:
                    d = d + g(s, 0) * sa
                y = (y + jnp.where(row_c >= s, g(s, 2), 0.0) * sa
                     + jnp.where(row_c >= s, g(s, 3), 0.0) * v[s:s + 1, :])
            y_ref[rows, lanes[p]] = y
            key, g_end, _ = keys[p]
            key_hi, key_lo = _split_bf16(key)
            val_hi, val_lo = _split_bf16(jnp.concatenate(sa_rows + [v], axis=0))
            g_hi = g_end.astype(BF16).astype(F32)
            g_rows = jnp.where(row_2c == 0, g_hi, jnp.where(row_2c == 1, g_end - g_hi, 0.0)).astype(BF16)
            operands.append((
                jnp.concatenate([key_hi, key_hi, key_lo, g_rows], axis=0),
                jnp.concatenate([
                    jnp.concatenate([val_hi, val_lo, val_hi, zero_2c], axis=0),
                    jnp.concatenate([zero_2c, zero_2c, zero_2c, ones_rows], axis=0)], axis=1)))
        for p in range(pairs):
            out = lax.dot_general(operands[p][0], operands[p][1], (((0,), (0,)), ((), ())),
                                  preferred_element_type=F32)
            s_scr[p] = own_head(out[:, LANES:]) * states[p] + own_head(out[:, :LANES])
        return carry

    lax.fori_loop(0, tt // c, chunk, 0)

    @pl.when(tb == n_tb - 1)
    def _():
        for p in range(pairs):
            st_ref[0, 2 * p] = s_scr[p][:, :RW_HEAD_DIM]
            st_ref[0, 2 * p + 1] = s_scr[p][:, RW_HEAD_DIM:]


def _rwkv_scan(r, w, k, v, kk, b, s0, *, n_seq, seq_len, tt):
    n_tb = seq_len // tt
    pairs = RW_HEADS // 2
    t_spec = pl.BlockSpec((tt, RW_WIDTH), lambda s, t: (s * n_tb + t, 0))
    s_spec = pl.BlockSpec((1, RW_HEADS, RW_HEAD_DIM, RW_HEAD_DIM), lambda s, t: (s, 0, 0, 0))
    return pl.pallas_call(
        functools.partial(_rwkv_scan_kernel, tt=tt, pairs=pairs),
        grid=(n_seq, n_tb),
        in_specs=[t_spec] * 6 + [s_spec],
        out_specs=[t_spec, s_spec],
        out_shape=[jax.ShapeDtypeStruct((n_seq * seq_len, RW_WIDTH), F32),
                   jax.ShapeDtypeStruct((n_seq, RW_HEADS, RW_HEAD_DIM, RW_HEAD_DIM), F32)],
        scratch_shapes=[pltpu.VMEM((pairs, RW_HEAD_DIM, LANES), F32)],
        compiler_params=_params(("parallel", "arbitrary")),
        name="rwkv_scan",
    )(r, w, k, v, kk, b, s0)


def _rwkv_post_kernel(y_ref, r_ref, k_ref, v_ref, g_ref, lnw_ref, lnb_ref, rk_ref, o_ref):
    q = _half_ones().astype(F32)
    seg = lambda x: jnp.dot(x, q, precision=HIGHEST, preferred_element_type=F32)
    for c in range(RW_WIDTH // LANES):
        sl = slice(c * LANES, (c + 1) * LANES)
        y = y_ref[:, sl]
        mean = seg(y) * (1.0 / RW_HEAD_DIM)
        d = y - mean
        var = seg(d * d) * (1.0 / RW_HEAD_DIM)
        yn = d * lax.rsqrt(var + RW_LN_EPS) * lnw_ref[:, sl] + lnb_ref[:, sl]
        bonus = seg(r_ref[:, sl] * k_ref[:, sl] * rk_ref[:, sl]) * v_ref[:, sl]
        o_ref[:, sl] = ((yn + bonus) * g_ref[:, sl]).astype(o_ref.dtype)


def _rwkv_post(y, r, k, v, g, lp, tt):
    m = y.shape[0]
    row = lambda a: a.reshape(1, -1)
    t_spec = pl.BlockSpec((tt, RW_WIDTH), lambda i: (i, 0))
    p_spec = pl.BlockSpec((1, RW_WIDTH), lambda i: (0, 0))
    return pl.pallas_call(
        _rwkv_post_kernel,
        grid=(m // tt,),
        in_specs=[t_spec] * 5 + [p_spec] * 3,
        out_specs=t_spec,
        out_shape=jax.ShapeDtypeStruct((m, RW_WIDTH), BF16),
        compiler_params=_params(("parallel",)),
        name="rwkv_post",
    )(y, r, k, v, g, row(lp['rwkv_ln_w']), row(lp['rwkv_ln_b']), row(lp['rwkv_r_k']))


BAND_TQ = 512


def _softmax2(s1, s2):
    m = jnp.maximum(jnp.max(s1, axis=-1, keepdims=True), jnp.max(s2, axis=-1, keepdims=True))
    p1 = jnp.exp(s1 - m)
    p2 = jnp.exp(s2 - m)
    denom = jnp.sum(p1, axis=-1, keepdims=True) + jnp.sum(p2, axis=-1, keepdims=True)
    return p1, p2, denom


def _dot_nt(a, b):
    return lax.dot_general(a, b, (((1,), (1,)), ((), ())), preferred_element_type=F32)


def _band_prompt_kernel(q_ref, kp_ref, kc_ref, vp_ref, vc_ref, bias_ref, o_ref):
    i = pl.program_id(1)
    scale = ATT_HEAD_DIM ** -0.5
    q = q_ref[...].astype(BF16)
    s1 = _dot_nt(q, kp_ref[...].astype(BF16)) * scale + bias_ref[:, :BAND_TQ]
    s2 = _dot_nt(q, kc_ref[...].astype(BF16)) * scale + bias_ref[:, BAND_TQ:]
    s1 = jnp.where(i == 0, NEG_INF, s1)
    p1, p2, denom = _softmax2(s1, s2)
    o = (jnp.dot(p1.astype(BF16), vp_ref[...].astype(BF16), preferred_element_type=F32)
         + jnp.dot(p2.astype(BF16), vc_ref[...].astype(BF16), preferred_element_type=F32))
    o_ref[...] = (o / denom).astype(o_ref.dtype)


def _band_prompt(qkv, bias):
    nq = SEQ // BAND_TQ
    prev = lambda h, i: jnp.maximum(i - 1, 0)
    blk = (BAND_TQ, ATT_HEAD_DIM)
    return pl.pallas_call(
        _band_prompt_kernel,
        grid=(ATT_HEADS, nq),
        in_specs=[
            pl.BlockSpec(blk, lambda h, i: (i, h)),
            pl.BlockSpec(blk, lambda h, i: (prev(h, i), ATT_HEADS + h)),
            pl.BlockSpec(blk, lambda h, i: (i, ATT_HEADS + h)),
            pl.BlockSpec(blk, lambda h, i: (prev(h, i), 2 * ATT_HEADS + h)),
            pl.BlockSpec(blk, lambda h, i: (i, 2 * ATT_HEADS + h)),
            pl.BlockSpec((None, BAND_TQ, 2 * BAND_TQ), lambda h, i: (h, 0, 0)),
        ],
        out_specs=pl.BlockSpec(blk, lambda h, i: (i, h)),
        out_shape=jax.ShapeDtypeStruct((SEQ, ATT_WIDTH), BF16),
        compiler_params=_params(("parallel", "parallel")),
        name="band_attention_prompt",
    )(qkv, qkv, qkv, qkv, qkv, bias)


def _band_prompt_bias(rel_bias):
    iq = np.arange(BAND_TQ)[:, None]
    ik = np.arange(2 * BAND_TQ)[None, :]
    period = 3 * BAND_TQ
    m = np.arange(period)
    k_minus_q = np.where(m <= 2 * BAND_TQ, m, m - period)
    idx = np.clip(BAND_TQ - k_minus_q, -REL_CLIP, REL_CLIP) + REL_CLIP
    vec = jnp.take(rel_bias, jnp.asarray(idx), axis=1)
    flat = jnp.broadcast_to(vec[:, None, :], (ATT_HEADS, BAND_TQ, period)).reshape(ATT_HEADS, -1)
    table = flat[:, :BAND_TQ * (period - 1)].reshape(ATT_HEADS, BAND_TQ, period - 1)[:, :, :2 * BAND_TQ]
    qc = iq // CHUNK + N_PREV_CHUNKS
    kc = ik // CHUNK
    mask = (kc <= qc) & (kc >= qc - N_PREV_CHUNKS)
    return jnp.where(jnp.asarray(mask)[None], table, NEG_INF).astype(F32)


def _band_sample_kernel(q_ref, k_ref, v_ref, kc_ref, vc_ref, b1_ref, b2_ref, o_ref):
    scale = ATT_HEAD_DIM ** -0.5
    for h in range(ATT_HEADS):
        sl = slice(h * ATT_HEAD_DIM, (h + 1) * ATT_HEAD_DIM)
        q = q_ref[:, sl].astype(BF16)
        s1 = _dot_nt(q, kc_ref[:, sl].astype(BF16)) * scale + b1_ref[h]
        s2 = _dot_nt(q, k_ref[:, sl].astype(BF16)) * scale + b2_ref[h]
        p1, p2, denom = _softmax2(s1, s2)
        o = (jnp.dot(p1.astype(BF16), vc_ref[:, sl].astype(BF16), preferred_element_type=F32)
             + jnp.dot(p2.astype(BF16), v_ref[:, sl].astype(BF16), preferred_element_type=F32))
        o_ref[:, sl] = (o / denom).astype(o_ref.dtype)


def _band_sample(qkv, cache_k, cache_v, bias_past, bias_new):
    row0 = SEQ // DEC_SEQ
    new = lambda c: pl.BlockSpec((DEC_SEQ, ATT_WIDTH), lambda b: (row0 + b, c))
    cache = pl.BlockSpec((None, ATT_PAST, ATT_WIDTH), lambda b: (b, 0, 0))
    return pl.pallas_call(
        _band_sample_kernel,
        grid=(DEC_BATCH,),
        in_specs=[new(0), new(1), new(2), cache, cache,
                  pl.BlockSpec((ATT_HEADS, DEC_SEQ, ATT_PAST), lambda b: (0, 0, 0)),
                  pl.BlockSpec((ATT_HEADS, DEC_SEQ, DEC_SEQ), lambda b: (0, 0, 0))],
        out_specs=pl.BlockSpec((DEC_SEQ, ATT_WIDTH), lambda b: (b, 0)),
        out_shape=jax.ShapeDtypeStruct((DEC_BATCH * DEC_SEQ, ATT_WIDTH), BF16),
        compiler_params=_params(("parallel",)),
        name="band_attention_sample",
    )(qkv, qkv, qkv, cache_k, cache_v, bias_past, bias_new)


def _band_sample_bias(rel_bias):
    q_pos = PAST_LEN + np.arange(DEC_SEQ)[:, None]
    k_pos = PAST_LEN - ATT_PAST + np.arange(ATT_PAST + DEC_SEQ)[None, :]
    rel = np.clip(q_pos - k_pos, -REL_CLIP, REL_CLIP) + REL_CLIP
    qc, kc = q_pos // CHUNK, k_pos // CHUNK
    mask = (k_pos >= 0) & (kc <= qc) & (kc >= qc - N_PREV_CHUNKS)
    table = jnp.take(rel_bias, jnp.asarray(rel.reshape(-1)), axis=1).reshape(
        ATT_HEADS, DEC_SEQ, ATT_PAST + DEC_SEQ)
    table = jnp.where(jnp.asarray(mask)[None], table, NEG_INF).astype(F32)
    return table[:, :, :ATT_PAST], table[:, :, ATT_PAST:]


def _mem_attn_kernel(q_ref, k_ref, v_ref, o_ref):
    s = _dot_nt(q_ref[...], k_ref[...].astype(BF16)) * (MEM_HEAD_DIM ** -0.5)
    m = jnp.max(s, axis=-1, keepdims=True)
    p = jnp.exp(s - m)
    denom = jnp.sum(p, axis=-1, keepdims=True)
    o = jnp.dot(p.astype(BF16), v_ref[...].astype(BF16), preferred_element_type=F32)
    o_ref[...] = (o / denom).astype(o_ref.dtype)


def _mem_attn(q_mem, mem_k, mem_v, *, row0, n_rows, tq, rows_per_mem):
    off = row0 // tq
    kv_spec = pl.BlockSpec((None, N_MEM, MEM_HEAD_DIM), lambda h, i: ((i * tq) // rows_per_mem, 0, h))
    return pl.pallas_call(
        _mem_attn_kernel,
        grid=(MEM_HEADS, n_rows // tq),
        in_specs=[pl.BlockSpec((tq, MEM_HEAD_DIM), lambda h, i: (off + i, h)), kv_spec, kv_spec],
        out_specs=pl.BlockSpec((tq, MEM_HEAD_DIM), lambda h, i: (i, h)),
        out_shape=jax.ShapeDtypeStruct((n_rows, MEM_WIDTH), BF16),
        compiler_params=_params(("parallel", "parallel")),
        name="memory_attention",
    )(q_mem, mem_k, mem_v)


ROUTER_COLS = 128


def _norm_router_kernel(x_ref, g_ref, wr_ref, br_ref, h_ref, id_ref, wt_ref):
    x = x_ref[...]
    ms = jnp.mean(x * x, axis=-1, keepdims=True)
    h = x * lax.rsqrt(ms + NORM_EPS) * g_ref[...]
    bits = lax.bitcast_convert_type(h.astype(BF16).astype(F32), jnp.uint32)
    half = bits.shape[1] // 2
    h_ref[...] = (bits[:, :half] >> 16) | bits[:, half:]
    logits = jnp.dot(h, wr_ref[...], precision=HIGHEST, preferred_element_type=F32) + br_ref[...]

    lane = lax.broadcasted_iota(jnp.int32, logits.shape, 1).astype(F32)
    row_max = lambda a: jnp.max(a, axis=-1, keepdims=True)
    first_at = lambda a, top: jnp.min(jnp.where(a == top, lane, float(ROUTER_COLS)), axis=-1, keepdims=True)
    is_group = lane < N_GROUPS
    g_logit = jnp.where(is_group, logits, NEG_INF)
    g_top = row_max(g_logit)
    g_idx = first_at(g_logit, g_top)
    p_group = 1.0 / jnp.sum(jnp.where(is_group, jnp.exp(logits - g_top), 0.0), axis=-1, keepdims=True)
    lo = N_GROUPS + g_idx * EXPERTS_PER_GROUP
    e_logit = jnp.where((lane >= lo) & (lane < lo + EXPERTS_PER_GROUP), logits, NEG_INF)
    e1 = row_max(e_logit)
    i1 = first_at(e_logit, e1)
    e_rest = jnp.where(lane == i1, NEG_INF, e_logit)
    e2 = row_max(e_rest)
    i2 = first_at(e_rest, e2)
    t = jnp.exp(e2 - e1)
    w1 = p_group / (1.0 + t)
    w2 = p_group * t / (1.0 + t)
    id_ref[...] = jnp.where(lane == 0, i1 - N_GROUPS, jnp.where(lane == 1, i2 - N_GROUPS, 0.0)).astype(jnp.int32)
    wt_ref[...] = jnp.where(lane == 0, w1, jnp.where(lane == 1, w2, 0.0))


def _norm_router(x, g, w_router, b_router, tm):
    m, d = x.shape
    return pl.pallas_call(
        _norm_router_kernel,
        grid=(m // tm,),
        in_specs=[pl.BlockSpec((tm, d), lambda i: (i, 0)),
                  pl.BlockSpec((1, d), lambda i: (0, 0)),
                  pl.BlockSpec((d, ROUTER_COLS), lambda i: (0, 0)),
                  pl.BlockSpec((1, ROUTER_COLS), lambda i: (0, 0))],
        out_specs=[pl.BlockSpec((tm, d // 2), lambda i: (i, 0)),
                   pl.BlockSpec((tm, ROUTER_COLS), lambda i: (i, 0)),
                   pl.BlockSpec((tm, ROUTER_COLS), lambda i: (i, 0))],
        out_shape=[jax.ShapeDtypeStruct((m, d // 2), jnp.uint32),
                   jax.ShapeDtypeStruct((m, ROUTER_COLS), jnp.int32),
                   jax.ShapeDtypeStruct((m, ROUTER_COLS), F32)],
        compiler_params=_params(("parallel",)),
        name="ffn_norm_router",
    )(x, g.reshape(1, d), w_router, b_router)


GATHER_WINDOW = 16


def _gather_rows_kernel(n_valid_ref, idx_ref, src_ref, init_ref, dst_ref, sem):
    del init_ref
    b = pl.program_id(0)
    n = n_valid_ref[b]
    base = b * MOE_ROWS

    def row_copy(r):
        return pltpu.make_async_copy(src_ref.at[pl.ds(idx_ref[0, r], 1)],
                                     dst_ref.at[pl.ds(base + r, 1)],
                                     sem.at[r % GATHER_WINDOW])

    def issue(r, carry):
        @pl.when(r >= GATHER_WINDOW)
        def _():
            row_copy(r - GATHER_WINDOW).wait()
        row_copy(r).start()
        return carry

    def drain(r, carry):
        row_copy(r).wait()
        return carry

    lax.fori_loop(0, n, issue, 0)
    lax.fori_loop(jnp.maximum(n - GATHER_WINDOW, 0), n, drain, 0)


def _gather_rows(src, idx, n_valid):
    nb = idx.shape[0]
    rows, width = nb * MOE_ROWS, src.shape[1]
    any_spec = pl.BlockSpec(memory_space=pl.ANY)
    grid_spec = pltpu.PrefetchScalarGridSpec(
        num_scalar_prefetch=1,
        grid=(nb,),
        in_specs=[pl.BlockSpec((None, 1, MOE_ROWS), lambda b, nv: (b, 0, 0), memory_space=pltpu.SMEM),
                  any_spec, any_spec],
        out_specs=any_spec,
        scratch_shapes=[pltpu.SemaphoreType.DMA((GATHER_WINDOW,))],
    )
    return pl.pallas_call(
        _gather_rows_kernel,
        grid_spec=grid_spec,
        out_shape=jax.ShapeDtypeStruct((rows, width), src.dtype),
        input_output_aliases={3: 0},
        compiler_params=_params(("arbitrary",)),
        name="moe_gather_rows",
    )(n_valid, idx, src, jnp.zeros((rows, width), src.dtype))


def _expert_kernel(blk_e_ref, blk_ok_ref, blk_src_ref, x_ref, wg_ref, wu_ref, wd_ref, o_ref, xb_ref):
    del blk_e_ref, blk_src_ref
    b = pl.program_id(0)
    f = pl.program_id(1)
    ok = blk_ok_ref[b] > 0
    half = D_MODEL // 2

    @pl.when(f == 0)
    def _():
        o_ref[...] = jnp.zeros_like(o_ref)

    @pl.when(ok & (f == 0))
    def _():
        u = x_ref[...]
        xb_ref[:, :half] = lax.bitcast_convert_type(u << 16, F32).astype(BF16)
        xb_ref[:, half:] = lax.bitcast_convert_type(u & jnp.uint32(0xFFFF0000), F32).astype(BF16)

    @pl.when(ok)
    def _():
        x = xb_ref[...]
        gate = jnp.dot(x, wg_ref[...].astype(BF16), preferred_element_type=F32)
        up = jnp.dot(x, wu_ref[...].astype(BF16), preferred_element_type=F32)
        hidden = (gate * _sigmoid(gate) * up).astype(BF16)
        for n in range(D_MODEL // MOE_DOWN_COLS):
            cols = slice(n * MOE_DOWN_COLS, (n + 1) * MOE_DOWN_COLS)
            o_ref[:, cols] += jnp.dot(hidden, wd_ref[:, cols].astype(BF16), preferred_element_type=F32)


def _expert_ffn(xs, blk_e, blk_ok, blk_src, w_gate, w_up, w_down):
    rows = xs.shape[0]
    nb = rows // MOE_ROWS
    nf = D_EXPERT // MOE_FCHUNK
    fidx = lambda b, f, ok: jnp.where(ok[b] > 0, f, nf - 1)
    grid_spec = pltpu.PrefetchScalarGridSpec(
        num_scalar_prefetch=3,
        grid=(nb, nf),
        in_specs=[
            pl.BlockSpec((MOE_ROWS, D_MODEL // 2), lambda b, f, e, ok, src: (src[b], 0)),
            pl.BlockSpec((None, D_MODEL, MOE_FCHUNK), lambda b, f, e, ok, src: (e[b], 0, fidx(b, f, ok))),
            pl.BlockSpec((None, D_MODEL, MOE_FCHUNK), lambda b, f, e, ok, src: (e[b], 0, fidx(b, f, ok))),
            pl.BlockSpec((None, MOE_FCHUNK, D_MODEL), lambda b, f, e, ok, src: (e[b], fidx(b, f, ok), 0)),
        ],
        out_specs=pl.BlockSpec((MOE_ROWS, D_MODEL), lambda b, f, e, ok, src: (b, 0)),
        scratch_shapes=[pltpu.VMEM((MOE_ROWS, D_MODEL), BF16)],
    )
    return pl.pallas_call(
        _expert_kernel,
        grid_spec=grid_spec,
        out_shape=jax.ShapeDtypeStruct((rows, D_MODEL), F32),
        compiler_params=_params(("arbitrary", "arbitrary"), MOE_VMEM_LIMIT),
        name="expert_ffn",
    )(blk_e, blk_ok, blk_src, xs, w_gate, w_up, w_down)


def _moe_layout(expert_id):
    n = expert_id.shape[0]
    a = n * TOP_K
    n_blocks = a // MOE_ROWS + N_EXPERTS
    rows = n_blocks * MOE_ROWS
    flat_e = expert_id.reshape(a)
    flat_tok = jnp.arange(a, dtype=jnp.int32) // TOP_K
    order = jnp.argsort(flat_e)
    sorted_e = flat_e[order]
    counts = jnp.bincount(flat_e, length=N_EXPERTS).astype(jnp.int32)
    start = jnp.cumsum(counts) - counts
    padded = (counts + MOE_ROWS - 1) // MOE_ROWS * MOE_ROWS
    pad_end = jnp.cumsum(padded)
    pad_start = pad_end - padded
    dest = (pad_start[sorted_e] + (jnp.arange(a, dtype=jnp.int32) - start[sorted_e])).astype(jnp.int32)
    row_tok = jnp.zeros((rows,), jnp.int32).at[dest].set(flat_tok[order])
    pos = jnp.zeros((a,), jnp.int32).at[order].set(dest).reshape(n, TOP_K)
    blk_start = jnp.arange(n_blocks, dtype=jnp.int32) * MOE_ROWS
    blk_ok = (blk_start < pad_end[-1]).astype(jnp.int32)
    n_ok = jnp.sum(blk_ok)
    blk_e = jnp.minimum(jnp.searchsorted(pad_end, blk_start, side='right'), N_EXPERTS - 1).astype(jnp.int32)
    blk_e = jnp.where(blk_ok > 0, blk_e, blk_e[n_ok - 1])
    blk_src = jnp.where(blk_ok > 0, jnp.arange(n_blocks, dtype=jnp.int32), n_ok - 1).astype(jnp.int32)
    n_valid = jnp.clip(counts[blk_e] - (blk_start - pad_start[blk_e]), 0, MOE_ROWS) * blk_ok
    return row_tok.reshape(n_blocks, 1, MOE_ROWS), n_valid.astype(jnp.int32), pos, blk_e, blk_ok, blk_src


def kernel(x_prompt, x_sample, mem_prompt, cache_att_k, cache_att_v, cache_mem_k, cache_mem_v, state_rwkv, state_rwkv_shift, norm_mix_g, w_in, rwkv_mu, rwkv_w0, rwkv_w2, rwkv_a0, rwkv_a2, rwkv_g2, rwkv_k_k, rwkv_k_a, rwkv_r_k, rwkv_ln_w, rwkv_ln_b, w_rwkv_out, att_q_g, att_k_g, att_rel_bias, w_att_out, mem_norm_g, w_mem_kv, mem_q_g, mem_k_g, w_mem_out, w_o, norm_ffn_g, w_router_group, b_router_group, w_router_expert, b_router_expert, w_exp_gate, w_exp_up, w_exp_down):
    lp = dict(rwkv_mu=rwkv_mu[0], rwkv_w0=rwkv_w0[0], rwkv_w2=rwkv_w2[0], rwkv_a0=rwkv_a0[0],
              rwkv_a2=rwkv_a2[0], rwkv_g2=rwkv_g2[0], rwkv_k_k=rwkv_k_k[0], rwkv_k_a=rwkv_k_a[0],
              rwkv_r_k=rwkv_r_k[0], rwkv_ln_w=rwkv_ln_w[0], rwkv_ln_b=rwkv_ln_b[0])
    n_samp = DEC_BATCH * DEC_SEQ
    x = jnp.concatenate([x_prompt.reshape(SEQ, D_MODEL), x_sample.reshape(n_samp, D_MODEL)], axis=0)

    h = _rmsnorm(x, norm_mix_g[0], 264, BF16)
    w_in0 = w_in[0]
    c1 = RW_COLS
    c2 = c1 + 3 * ATT_WIDTH
    c3 = c2 + MEM_WIDTH
    w_rw = jnp.pad(w_in0[:, :c1].astype(BF16), ((0, 0), (0, RW_COLS_PAD - RW_COLS)))
    z_rw = _matmul(_mm_plain_kernel, h, w_rw, (), (), tm=1056, tn=512, out_dtype=F32, name="proj_rwkv")

    qk_gain = jnp.concatenate([jnp.tile(att_q_g[0], ATT_HEADS), jnp.tile(att_k_g[0], ATT_HEADS),
                               jnp.ones((ATT_WIDTH,), F32)]).reshape(1, 3 * ATT_WIDTH)
    qkv = _matmul(functools.partial(_mm_headnorm_kernel, head=ATT_HEAD_DIM, n_norm_tiles=8),
                  h, w_in0[:, c1:c2].astype(BF16), (qk_gain,),
                  (pl.BlockSpec((1, 512), lambda j, i: (0, j)),),
                  tm=1056, tn=512, out_dtype=F32, name="proj_qkv")
    q_mem = _matmul(functools.partial(_mm_headnorm_kernel, head=MEM_HEAD_DIM, n_norm_tiles=MEM_HEADS),
                    h, w_in0[:, c2:c3].astype(BF16), (jnp.tile(mem_q_g[0], MEM_HEADS).reshape(1, MEM_WIDTH),),
                    (pl.BlockSpec((1, 512), lambda j, i: (0, j)),),
                    tm=1056, tn=512, out_dtype=BF16, name="proj_mem_q")
    gate = _matmul(_mm_sigmoid_kernel, h, w_in0[:, c3:].astype(BF16), (), (),
                   tm=1056, tn=512, out_dtype=BF16, name="proj_gate")

    shift_p = jnp.zeros((1, 1, RW_COLS), F32)
    prep_p = _rwkv_prep(z_rw, shift_p, lp, row0=0, n_seq=1, seq_len=SEQ, tt=128)
    prep_s = _rwkv_prep(z_rw, state_rwkv_shift[0], lp, row0=SEQ, n_seq=DEC_BATCH, seq_len=DEC_SEQ,
                        tt=DEC_SEQ)
    s0_p = jnp.zeros((1, RW_HEADS, RW_HEAD_DIM, RW_HEAD_DIM), F32)
    rw_parts = []
    states = []
    for (r, w, k, v, kk, b, g), s0, n_seq, seq_len, tt in (
            (prep_p, s0_p, 1, SEQ, 128), (prep_s, state_rwkv[0], DEC_BATCH, DEC_SEQ, DEC_SEQ)):
        y, s_new = _rwkv_scan(r, w, k, v, kk, b, jnp.swapaxes(s0, -1, -2),
                              n_seq=n_seq, seq_len=seq_len, tt=tt)
        rw_parts.append(_rwkv_post(y, r, k, v, g, lp, 256))
        states.append(jnp.swapaxes(s_new, -1, -2))
    a_rw = jnp.concatenate(rw_parts, axis=0)

    bias_p = _band_prompt_bias(att_rel_bias[0])
    bias_s1, bias_s2 = _band_sample_bias(att_rel_bias[0])
    att_p = _band_prompt(qkv, bias_p)
    att_s = _band_sample(qkv, cache_att_k[0].reshape(DEC_BATCH, ATT_PAST, ATT_WIDTH),
                         cache_att_v[0].reshape(DEC_BATCH, ATT_PAST, ATT_WIDTH), bias_s1, bias_s2)
    a_att = jnp.concatenate([att_p, att_s], axis=0)

    mem_h = _rmsnorm(mem_prompt.reshape(N_MEM, D_MODEL), mem_norm_g[0], N_MEM, BF16)
    w_kv = w_mem_kv[0]
    mk_p = _matmul(functools.partial(_mm_headnorm_kernel, head=MEM_HEAD_DIM, n_norm_tiles=MEM_HEADS),
                   mem_h, w_kv[:, :MEM_WIDTH].astype(BF16),
                   (jnp.tile(mem_k_g[0], MEM_HEADS).reshape(1, MEM_WIDTH),),
                   (pl.BlockSpec((1, 512), lambda j, i: (0, j)),),
                   tm=N_MEM, tn=512, out_dtype=F32, name="mem_k")
    mv_p = _matmul(_mm_plain_kernel, mem_h, w_kv[:, MEM_WIDTH:].astype(BF16), (), (),
                   tm=N_MEM, tn=512, out_dtype=F32, name="mem_v")
    mem_p = _mem_attn(q_mem, mk_p.reshape(1, N_MEM, MEM_WIDTH), mv_p.reshape(1, N_MEM, MEM_WIDTH),
                      row0=0, n_rows=SEQ, tq=512, rows_per_mem=SEQ)
    mem_s = _mem_attn(q_mem, cache_mem_k[0].reshape(DEC_BATCH, N_MEM, MEM_WIDTH),
                      cache_mem_v[0].reshape(DEC_BATCH, N_MEM, MEM_WIDTH),
                      row0=SEQ, n_rows=n_samp, tq=DEC_SEQ, rows_per_mem=DEC_SEQ)
    a_mem = jnp.concatenate([mem_p, mem_s], axis=0)

    merged = _gated_merge(a_rw, a_att, a_mem, w_rwkv_out[0].astype(BF16), w_att_out[0].astype(BF16),
                          w_mem_out[0].astype(BF16), gate, 528, 512)
    x1 = _matmul(_mm_residual_kernel, merged, w_o[0].astype(BF16), (x,),
                 (pl.BlockSpec((1056, 512), lambda j, i: (i, j)),),
                 tm=1056, tn=512, out_dtype=F32, name="proj_out")

    w_router = jnp.zeros((D_MODEL, ROUTER_COLS), F32)
    w_router = w_router.at[:, :N_GROUPS].set(w_router_group[0])
    w_router = w_router.at[:, N_GROUPS:N_GROUPS + N_EXPERTS].set(w_router_expert[0])
    b_router = jnp.zeros((1, ROUTER_COLS), F32)
    b_router = b_router.at[0, :N_GROUPS].set(b_router_group[0])
    b_router = b_router.at[0, N_GROUPS:N_GROUPS + N_EXPERTS].set(b_router_expert[0])
    h2, route_id, route_w = _norm_router(x1, norm_ffn_g[0], w_router, b_router, 264)
    row_tok, n_valid, pos, blk_e, blk_ok, blk_src = _moe_layout(route_id[:, :TOP_K])
    yb = _expert_ffn(_gather_rows(h2, row_tok, n_valid), blk_e, blk_ok, blk_src,
                     w_exp_gate[0], w_exp_up[0], w_exp_down[0])
    take_rows = lambda idx: jnp.take(yb, idx, axis=0, mode="clip")

    def combine(rows):
        return x1[rows] + (route_w[rows, 0:1] * take_rows(pos[rows, 0])
                           + route_w[rows, 1:2] * take_rows(pos[rows, 1]))

    y_prompt = combine(slice(0, SEQ))
    y_sample = combine(slice(SEQ, N_TOK))

    heads = lambda t, bsz, rows: t.reshape(1, bsz, rows, ATT_HEADS, ATT_HEAD_DIM)
    k_cols = slice(ATT_WIDTH, 2 * ATT_WIDTH)
    v_cols = slice(2 * ATT_WIDTH, 3 * ATT_WIDTH)
    keep = SEQ - N_PREV_CHUNKS * CHUNK
    z_s = z_rw[SEQ:, :RW_COLS].reshape(DEC_BATCH, DEC_SEQ, RW_COLS)
    return (
        y_prompt.reshape(1, SEQ, D_MODEL),
        y_sample.reshape(DEC_BATCH, DEC_SEQ, D_MODEL),
        heads(qkv[keep:SEQ, k_cols], 1, SEQ - keep),
        heads(qkv[keep:SEQ, v_cols], 1, SEQ - keep),
        states[0].reshape(1, 1, RW_HEADS, RW_HEAD_DIM, RW_HEAD_DIM),
        z_rw[SEQ - 1:SEQ, :RW_COLS].reshape(1, 1, 1, RW_COLS),
        mk_p.reshape(1, 1, N_MEM, MEM_HEADS, MEM_HEAD_DIM),
        mv_p.reshape(1, 1, N_MEM, MEM_HEADS, MEM_HEAD_DIM),
        heads(qkv[SEQ:, k_cols], DEC_BATCH, DEC_SEQ),
        heads(qkv[SEQ:, v_cols], DEC_BATCH, DEC_SEQ),
        states[1].reshape(1, DEC_BATCH, RW_HEADS, RW_HEAD_DIM, RW_HEAD_DIM),
        z_s[:, -1:, :].reshape(1, DEC_BATCH, 1, RW_COLS),
    )
```

```python
import functools
import math

import numpy as np
import jax
import jax.numpy as jnp
from jax import lax
from jax.experimental import pallas as pl
from jax.experimental.pallas import tpu as pltpu

F32 = jnp.float32
BF16 = jnp.bfloat16
HIGHEST = lax.Precision.HIGHEST

D_MODEL = 4096
SEQ = 8192
DEC_BATCH = 16
DEC_SEQ = 16
PAST_LEN = 2048
N_TOK = SEQ + DEC_BATCH * DEC_SEQ
CHUNK = 64
N_PREV_CHUNKS = 8
RW_HEADS = 32
RW_HEAD_DIM = 64
RW_WIDTH = 2048
DECAY_LORA = 64
ICL_LORA = 64
GATE_LORA = 256
RW_LN_EPS = 64e-5
RW_COLS = 3 * RW_WIDTH + DECAY_LORA + ICL_LORA + GATE_LORA
RW_COLS_PAD = 6656
ATT_HEADS = 16
ATT_HEAD_DIM = 128
ATT_WIDTH = 2048
ATT_PAST = 512
REL_CLIP = 256
N_MEM = 256
MEM_HEADS = 4
MEM_HEAD_DIM = 512
MEM_WIDTH = 2048
N_GROUPS = 8
EXPERTS_PER_GROUP = 8
N_EXPERTS = 64
TOP_K = 2
D_EXPERT = 1024
NORM_EPS = 1e-6
NEG_INF = -1e30

LANES = 128
VMEM_LIMIT = 48 * 1024 * 1024
MOE_VMEM_LIMIT = 54 * 1024 * 1024
MOE_ROWS = 256
MOE_FCHUNK = 256
MOE_DOWN_COLS = 1024


def _params(sem, limit=VMEM_LIMIT):
    return pltpu.CompilerParams(dimension_semantics=sem, vmem_limit_bytes=limit)


def _sigmoid(x):
    return 1.0 / (1.0 + jnp.exp(-x))


def _rmsnorm_kernel(x_ref, g_ref, o_ref):
    x = x_ref[...]
    ms = jnp.mean(x * x, axis=-1, keepdims=True)
    o_ref[...] = (x * lax.rsqrt(ms + NORM_EPS) * g_ref[...]).astype(o_ref.dtype)


def _rmsnorm(x, g, tm, out_dtype):
    m, d = x.shape
    return pl.pallas_call(
        _rmsnorm_kernel,
        grid=(m // tm,),
        in_specs=[pl.BlockSpec((tm, d), lambda i: (i, 0)),
                  pl.BlockSpec((1, d), lambda i: (0, 0))],
        out_specs=pl.BlockSpec((tm, d), lambda i: (i, 0)),
        out_shape=jax.ShapeDtypeStruct((m, d), out_dtype),
        compiler_params=_params(("parallel",)),
        name="rmsnorm",
    )(x, g.reshape(1, d))


def _mm_plain_kernel(x_ref, w_ref, o_ref):
    o_ref[...] = jnp.dot(x_ref[...], w_ref[...], preferred_element_type=F32).astype(o_ref.dtype)


def _mm_headnorm_kernel(x_ref, w_ref, g_ref, o_ref, *, head, n_norm_tiles):
    acc = jnp.dot(x_ref[...], w_ref[...], preferred_element_type=F32)
    j = pl.program_id(0)
    tn = acc.shape[1]

    @pl.when(j < n_norm_tiles)
    def _():
        for c in range(tn // head):
            blk = acc[:, c * head:(c + 1) * head]
            ms = jnp.mean(blk * blk, axis=-1, keepdims=True)
            o_ref[:, c * head:(c + 1) * head] = (
                blk * lax.rsqrt(ms + NORM_EPS) * g_ref[:, c * head:(c + 1) * head]).astype(o_ref.dtype)

    @pl.when(j >= n_norm_tiles)
    def _():
        o_ref[...] = acc.astype(o_ref.dtype)


def _mm_sigmoid_kernel(x_ref, w_ref, o_ref):
    acc = jnp.dot(x_ref[...], w_ref[...], preferred_element_type=F32)
    o_ref[...] = _sigmoid(acc).astype(o_ref.dtype)


def _mm_residual_kernel(x_ref, w_ref, r_ref, o_ref):
    o_ref[...] = r_ref[...] + jnp.dot(x_ref[...], w_ref[...], preferred_element_type=F32)


def _matmul(body, x, w, extras, extra_specs, *, tm, tn, out_dtype, name):
    m, k = x.shape
    n = w.shape[1]
    return pl.pallas_call(
        body,
        grid=(n // tn, m // tm),
        in_specs=[pl.BlockSpec((tm, k), lambda j, i: (i, 0)),
                  pl.BlockSpec((k, tn), lambda j, i: (0, j))] + list(extra_specs),
        out_specs=pl.BlockSpec((tm, tn), lambda j, i: (i, j)),
        out_shape=jax.ShapeDtypeStruct((m, n), out_dtype),
        compiler_params=_params(("parallel", "parallel")),
        name=name,
    )(x, w, *extras)


def _merge_kernel(a0_ref, a1_ref, a2_ref, w0_ref, w1_ref, w2_ref, g0_ref, g1_ref, g2_ref, o_ref):
    y0 = jnp.dot(a0_ref[...], w0_ref[...], preferred_element_type=F32)
    y1 = jnp.dot(a1_ref[...], w1_ref[...], preferred_element_type=F32)
    y2 = jnp.dot(a2_ref[...], w2_ref[...], preferred_element_type=F32)
    merged = (g0_ref[...].astype(F32) * y0 + g1_ref[...].astype(F32) * y1
              + g2_ref[...].astype(F32) * y2)
    o_ref[...] = merged.astype(o_ref.dtype)


def _gated_merge(a_rw, a_att, a_mem, w_rw, w_att, w_mem, gate, tm, tn):
    m, k = a_rw.shape
    n = w_rw.shape[1]
    nj = n // tn
    a_spec = pl.BlockSpec((tm, k), lambda j, i: (i, 0))
    w_spec = pl.BlockSpec((k, tn), lambda j, i: (0, j))
    g_specs = [pl.BlockSpec((tm, tn), functools.partial(lambda j, i, b: (i, b * nj + j), b=b))
               for b in range(3)]
    return pl.pallas_call(
        _merge_kernel,
        grid=(nj, m // tm),
        in_specs=[a_spec, a_spec, a_spec, w_spec, w_spec, w_spec] + g_specs,
        out_specs=pl.BlockSpec((tm, tn), lambda j, i: (i, j)),
        out_shape=jax.ShapeDtypeStruct((m, n), BF16),
        compiler_params=_params(("parallel", "parallel")),
        name="gated_merge",
    )(a_rw, a_att, a_mem, w_rw, w_att, w_mem, gate, gate, gate)


def _half_ones():
    qi = lax.broadcasted_iota(jnp.int32, (LANES, LANES), 0) // RW_HEAD_DIM
    qj = lax.broadcasted_iota(jnp.int32, (LANES, LANES), 1) // RW_HEAD_DIM
    return qi == qj


def _rwkv_prep_kernel(z_ref, zp_ref, sh_ref, mu_ref, w0_ref, a0_ref, kk_ref, ka_ref,
                      w2_ref, a2_ref, g2_ref,
                      r_o, w_o, k_o, v_o, kk_o, b_o, g_o, *, tiles_per_seq):
    i = pl.program_id(0)
    first = (i % tiles_per_seq) == 0
    z = z_ref[...]
    prev_row = jnp.where(first, sh_ref[...], zp_ref[7:8, :])
    rows = lax.broadcasted_iota(jnp.int32, z.shape, 0)
    z_prev = jnp.where(rows == 0, prev_row, pltpu.roll(z, 1, 0))
    zs = z + (z_prev - z) * mu_ref[...]

    c1, c2, c3 = RW_WIDTH, 2 * RW_WIDTH, 3 * RW_WIDTH
    c4, c5 = c3 + DECAY_LORA, c3 + DECAY_LORA + ICL_LORA
    r = zs[:, :c1]
    k = zs[:, c1:c2]
    v = zs[:, c2:c3]
    wl = w0_ref[...] + jnp.dot(jnp.tanh(zs[:, c3:c4]), w2_ref[...], precision=HIGHEST,
                               preferred_element_type=F32)
    decay = jnp.exp(-math.exp(-0.5) * _sigmoid(wl))
    a = _sigmoid(a0_ref[...] + jnp.dot(zs[:, c4:c5], a2_ref[...], precision=HIGHEST,
                                       preferred_element_type=F32))
    g = jnp.dot(_sigmoid(zs[:, c5:]), g2_ref[...], precision=HIGHEST, preferred_element_type=F32)

    q = _half_ones().astype(F32)
    kk = k * kk_ref[...]
    for c in range(RW_WIDTH // LANES):
        sl = slice(c * LANES, (c + 1) * LANES)
        blk = kk[:, sl]
        ss = jnp.dot(blk * blk, q, precision=HIGHEST, preferred_element_type=F32)
        blk = blk / jnp.maximum(jnp.sqrt(ss), 1e-12)
        kk_o[:, sl] = blk
        b_o[:, sl] = blk * a[:, sl]
    r_o[...] = r
    w_o[...] = decay
    k_o[...] = k * (1.0 + (a - 1.0) * ka_ref[...])
    v_o[...] = v
    g_o[...] = g


def _rwkv_prep(z_rw, shift, lp, *, row0, n_seq, seq_len, tt):
    tiles_per_seq = seq_len // tt
    n_tiles = n_seq * tiles_per_seq
    off = row0 // tt
    off8 = row0 // 8
    row = lambda a: a.reshape(1, -1)
    full = lambda shape: pl.BlockSpec(shape, lambda i: (0, 0))
    out_sd = jax.ShapeDtypeStruct((n_seq * seq_len, RW_WIDTH), F32)
    o_spec = pl.BlockSpec((tt, RW_WIDTH), lambda i: (i, 0))
    return pl.pallas_call(
        functools.partial(_rwkv_prep_kernel, tiles_per_seq=tiles_per_seq),
        grid=(n_tiles,),
        in_specs=[
            pl.BlockSpec((tt, RW_COLS), lambda i: (off + i, 0)),
            pl.BlockSpec((8, RW_COLS), lambda i: (jnp.maximum(off8 + i * (tt // 8) - 1, 0), 0)),
            pl.BlockSpec((None, 1, RW_COLS), lambda i: (i // tiles_per_seq, 0, 0)),
            full((1, RW_COLS)), full((1, RW_WIDTH)), full((1, RW_WIDTH)), full((1, RW_WIDTH)),
            full((1, RW_WIDTH)), full((DECAY_LORA, RW_WIDTH)), full((ICL_LORA, RW_WIDTH)),
            full((GATE_LORA, RW_WIDTH)),
        ],
        out_specs=[o_spec] * 7,
        out_shape=[out_sd] * 7,
        compiler_params=_params(("parallel",)),
        name="rwkv_prep",
    )(z_rw, z_rw, shift, row(lp['rwkv_mu']), row(lp['rwkv_w0']), row(lp['rwkv_a0']),
      row(lp['rwkv_k_k']), row(lp['rwkv_k_a']), lp['rwkv_w2'], lp['rwkv_a2'], lp['rwkv_g2'])


RW_CHUNK = 8


def _split_bf16(x):
    hi = x.astype(BF16)
    return hi, (x - hi.astype(F32)).astype(BF16)


def _rwkv_scan_kernel(r_ref, w_ref, k_ref, v_ref, kk_ref, b_ref, s0_ref, y_ref, st_ref, s_scr,
                      *, tt, pairs):
    c = RW_CHUNK
    tb = pl.program_id(1)
    n_tb = pl.num_programs(1)
    left64 = lax.broadcasted_iota(jnp.int32, (RW_HEAD_DIM, LANES), 1) < RW_HEAD_DIM
    row_c = lax.broadcasted_iota(jnp.int32, (c, LANES), 0)
    q = _half_ones().astype(BF16)
    q2 = jnp.concatenate([q, q], axis=0)
    row_2c = lax.broadcasted_iota(jnp.int32, (2 * c, LANES), 0)
    ones_rows = jnp.where(row_2c < 2, 1.0, 0.0).astype(BF16)

    def segsum(x):
        hi, lo = _split_bf16(x)
        return jnp.dot(jnp.concatenate([hi, lo], axis=1), q2, preferred_element_type=F32)

    def shift_rows(x, n, fill):
        return jnp.where(row_c >= n, pltpu.roll(x, n, 0), fill)

    def per_head_blocks(x):
        zero = jnp.zeros_like(x)
        return jnp.concatenate([jnp.where(left64, x, zero), jnp.where(left64, zero, x)], axis=0)

    def own_head(x):
        return jnp.where(left64, x[:RW_HEAD_DIM], x[RW_HEAD_DIM:])

    @pl.when(tb == 0)
    def _():
        for p in range(pairs):
            s_scr[p] = jnp.concatenate([s0_ref[0, 2 * p], s0_ref[0, 2 * p + 1]], axis=1)

    zero_2c = jnp.zeros((2 * c, LANES), BF16)

    def chunk(ci, carry):
        rows = pl.ds(pl.multiple_of(ci * c, c), c)
        lanes = [pl.ds(p * LANES, LANES) for p in range(pairs)]
        vs, keys, states, grams, from_state = [], [], [], [], []
        for p in range(pairs):
            r, w, k, v, kk, b = (ref[rows, lanes[p]] for ref in (r_ref, w_ref, k_ref, v_ref, kk_ref, b_ref))
            gam = w
            n = 1
            while n < c:
                gam = gam * shift_rows(gam, n, 1.0)
                n *= 2
            inv = 1.0 / gam
            a_t = -(shift_rows(gam, 1, 1.0) * kk)
            b_t = b * inv
            k_t = k * inv
            r_t = gam * r
            g_end = gam[c - 1:c, :]
            prods = []
            for s in range(c):
                bs, ks = b_t[s:s + 1, :], k_t[s:s + 1, :]
                prods += [a_t * bs, a_t * ks, r_t * bs, r_t * ks]
            grams.append(segsum(jnp.concatenate(prods, axis=0)))
            vs.append(v)
            keys.append((jnp.concatenate([b_t, k_t], axis=0) * g_end, g_end,
                         _split_bf16(jnp.concatenate([a_t, r_t], axis=0))))
        for p in range(pairs):
            st = s_scr[p]
            s_hi, s_lo = _split_bf16(st)
            ar_hi, ar_lo = keys[p][2]
            states.append(st)
            from_state.append(jnp.dot(
                jnp.concatenate([ar_hi, ar_hi, ar_lo], axis=1),
                jnp.concatenate([per_head_blocks(s_hi), per_head_blocks(s_lo), per_head_blocks(s_hi)], axis=0),
                preferred_element_type=F32))
        operands = []
        for p in range(pairs):
            v = vs[p]
            g = lambda s, m: grams[p][(4 * s + m) * c:(4 * s + m + 1) * c, :]
            d = from_state[p][:c]
            y = from_state[p][c:]
            for s in range(c - 1):
                d = d + jnp.where(row_c > s, g(s, 1), 0.0) * v[s:s + 1, :]
            sa_rows = []
            for s in range(c):
                sa = d[s:s + 1, :]
                sa_rows.append(sa)
                if s < c - 1:
                    d = d + g(s, 0) * sa
                y = (y + jnp.where(row_c >= s, g(s, 2), 0.0) * sa
                     + jnp.where(row_c >= s, g(s, 3), 0.0) * v[s:s + 1, :])
            y_ref[rows, lanes[p]] = y
            key, g_end, _ = keys[p]
            key_hi, key_lo = _split_bf16(key)
            val_hi, val_lo = _split_bf16(jnp.concatenate(sa_rows + [v], axis=0))
            g_hi = g_end.astype(BF16).astype(F32)
            g_rows = jnp.where(row_2c == 0, g_hi, jnp.where(row_2c == 1, g_end - g_hi, 0.0)).astype(BF16)
            operands.append((
                jnp.concatenate([key_hi, key_hi, key_lo, g_rows], axis=0),
                jnp.concatenate([
                    jnp.concatenate([val_hi, val_lo, val_hi, zero_2c], axis=0),
                    jnp.concatenate([zero_2c, zero_2c, zero_2c, ones_rows], axis=0)], axis=1)))
        for p in range(pairs):
            out = lax.dot_general(operands[p][0], operands[p][1], (((0,), (0,)), ((), ())),
                                  preferred_element_type=F32)
            s_scr[p] = own_head(out[:, LANES:]) * states[p] + own_head(out[:, :LANES])
        return carry

    lax.fori_loop(0, tt // c, chunk, 0)

    @pl.when(tb == n_tb - 1)
    def _():
        for p in range(pairs):
            st_ref[0, 2 * p] = s_scr[p][:, :RW_HEAD_DIM]
            st_ref[0, 2 * p + 1] = s_scr[p][:, RW_HEAD_DIM:]


def _rwkv_scan(r, w, k, v, kk, b, s0, *, n_seq, seq_len, tt):
    n_tb = seq_len // tt
    pairs = RW_HEADS // 2
    t_spec = pl.BlockSpec((tt, RW_WIDTH), lambda s, t: (s * n_tb + t, 0))
    s_spec = pl.BlockSpec((1, RW_HEADS, RW_HEAD_DIM, RW_HEAD_DIM), lambda s, t: (s, 0, 0, 0))
    return pl.pallas_call(
        functools.partial(_rwkv_scan_kernel, tt=tt, pairs=pairs),
        grid=(n_seq, n_tb),
        in_specs=[t_spec] * 6 + [s_spec],
        out_specs=[t_spec, s_spec],
        out_shape=[jax.ShapeDtypeStruct((n_seq * seq_len, RW_WIDTH), F32),
                   jax.ShapeDtypeStruct((n_seq, RW_HEADS, RW_HEAD_DIM, RW_HEAD_DIM), F32)],
        scratch_shapes=[pltpu.VMEM((pairs, RW_HEAD_DIM, LANES), F32)],
        compiler_params=_params(("parallel", "arbitrary")),
        name="rwkv_scan",
    )(r, w, k, v, kk, b, s0)


def _rwkv_post_kernel(y_ref, r_ref, k_ref, v_ref, g_ref, lnw_ref, lnb_ref, rk_ref, o_ref):
    q = _half_ones().astype(F32)
    seg = lambda x: jnp.dot(x, q, precision=HIGHEST, preferred_element_type=F32)
    for c in range(RW_WIDTH // LANES):
        sl = slice(c * LANES, (c + 1) * LANES)
        y = y_ref[:, sl]
        mean = seg(y) * (1.0 / RW_HEAD_DIM)
        d = y - mean
        var = seg(d * d) * (1.0 / RW_HEAD_DIM)
        yn = d * lax.rsqrt(var + RW_LN_EPS) * lnw_ref[:, sl] + lnb_ref[:, sl]
        bonus = seg(r_ref[:, sl] * k_ref[:, sl] * rk_ref[:, sl]) * v_ref[:, sl]
        o_ref[:, sl] = ((yn + bonus) * g_ref[:, sl]).astype(o_ref.dtype)


def _rwkv_post(y, r, k, v, g, lp, tt):
    m = y.shape[0]
    row = lambda a: a.reshape(1, -1)
    t_spec = pl.BlockSpec((tt, RW_WIDTH), lambda i: (i, 0))
    p_spec = pl.BlockSpec((1, RW_WIDTH), lambda i: (0, 0))
    return pl.pallas_call(
        _rwkv_post_kernel,
        grid=(m // tt,),
        in_specs=[t_spec] * 5 + [p_spec] * 3,
        out_specs=t_spec,
        out_shape=jax.ShapeDtypeStruct((m, RW_WIDTH), BF16),
        compiler_params=_params(("parallel",)),
        name="rwkv_post",
    )(y, r, k, v, g, row(lp['rwkv_ln_w']), row(lp['rwkv_ln_b']), row(lp['rwkv_r_k']))


BAND_TQ = 512


def _softmax2(s1, s2):
    m = jnp.maximum(jnp.max(s1, axis=-1, keepdims=True), jnp.max(s2, axis=-1, keepdims=True))
    p1 = jnp.exp(s1 - m)
    p2 = jnp.exp(s2 - m)
    denom = jnp.sum(p1, axis=-1, keepdims=True) + jnp.sum(p2, axis=-1, keepdims=True)
    return p1, p2, denom


def _dot_nt(a, b):
    return lax.dot_general(a, b, (((1,), (1,)), ((), ())), preferred_element_type=F32)


def _band_prompt_kernel(q_ref, kp_ref, kc_ref, vp_ref, vc_ref, bias_ref, o_ref):
    i = pl.program_id(1)
    scale = ATT_HEAD_DIM ** -0.5
    q = q_ref[...].astype(BF16)
    s1 = _dot_nt(q, kp_ref[...].astype(BF16)) * scale + bias_ref[:, :BAND_TQ]
    s2 = _dot_nt(q, kc_ref[...].astype(BF16)) * scale + bias_ref[:, BAND_TQ:]
    s1 = jnp.where(i == 0, NEG_INF, s1)
    p1, p2, denom = _softmax2(s1, s2)
    o = (jnp.dot(p1.astype(BF16), vp_ref[...].astype(BF16), preferred_element_type=F32)
         + jnp.dot(p2.astype(BF16), vc_ref[...].astype(BF16), preferred_element_type=F32))
    o_ref[...] = (o / denom).astype(o_ref.dtype)


def _band_prompt(qkv, bias):
    nq = SEQ // BAND_TQ
    prev = lambda h, i: jnp.maximum(i - 1, 0)
    blk = (BAND_TQ, ATT_HEAD_DIM)
    return pl.pallas_call(
        _band_prompt_kernel,
        grid=(ATT_HEADS, nq),
        in_specs=[
            pl.BlockSpec(blk, lambda h, i: (i, h)),
            pl.BlockSpec(blk, lambda h, i: (prev(h, i), ATT_HEADS + h)),
            pl.BlockSpec(blk, lambda h, i: (i, ATT_HEADS + h)),
            pl.BlockSpec(blk, lambda h, i: (prev(h, i), 2 * ATT_HEADS + h)),
            pl.BlockSpec(blk, lambda h, i: (i, 2 * ATT_HEADS + h)),
            pl.BlockSpec((None, BAND_TQ, 2 * BAND_TQ), lambda h, i: (h, 0, 0)),
        ],
        out_specs=pl.BlockSpec(blk, lambda h, i: (i, h)),
        out_shape=jax.ShapeDtypeStruct((SEQ, ATT_WIDTH), BF16),
        compiler_params=_params(("parallel", "parallel")),
        name="band_attention_prompt",
    )(qkv, qkv, qkv, qkv, qkv, bias)


def _band_prompt_bias(rel_bias):
    iq = np.arange(BAND_TQ)[:, None]
    ik = np.arange(2 * BAND_TQ)[None, :]
    period = 3 * BAND_TQ
    m = np.arange(period)
    k_minus_q = np.where(m <= 2 * BAND_TQ, m, m - period)
    idx = np.clip(BAND_TQ - k_minus_q, -REL_CLIP, REL_CLIP) + REL_CLIP
    vec = jnp.take(rel_bias, jnp.asarray(idx), axis=1)
    flat = jnp.broadcast_to(vec[:, None, :], (ATT_HEADS, BAND_TQ, period)).reshape(ATT_HEADS, -1)
    table = flat[:, :BAND_TQ * (period - 1)].reshape(ATT_HEADS, BAND_TQ, period - 1)[:, :, :2 * BAND_TQ]
    qc = iq // CHUNK + N_PREV_CHUNKS
    kc = ik // CHUNK
    mask = (kc <= qc) & (kc >= qc - N_PREV_CHUNKS)
    return jnp.where(jnp.asarray(mask)[None], table, NEG_INF).astype(F32)


def _band_sample_kernel(q_ref, k_ref, v_ref, kc_ref, vc_ref, b1_ref, b2_ref, o_ref):
    scale = ATT_HEAD_DIM ** -0.5
    for h in range(ATT_HEADS):
        sl = slice(h * ATT_HEAD_DIM, (h + 1) * ATT_HEAD_DIM)
        q = q_ref[:, sl].astype(BF16)
        s1 = _dot_nt(q, kc_ref[:, sl].astype(BF16)) * scale + b1_ref[h]
        s2 = _dot_nt(q, k_ref[:, sl].astype(BF16)) * scale + b2_ref[h]
        p1, p2, denom = _softmax2(s1, s2)
        o = (jnp.dot(p1.astype(BF16), vc_ref[:, sl].astype(BF16), preferred_element_type=F32)
             + jnp.dot(p2.astype(BF16), v_ref[:, sl].astype(BF16), preferred_element_type=F32))
        o_ref[:, sl] = (o / denom).astype(o_ref.dtype)


def _band_sample(qkv, cache_k, cache_v, bias_past, bias_new):
    row0 = SEQ // DEC_SEQ
    new = lambda c: pl.BlockSpec((DEC_SEQ, ATT_WIDTH), lambda b: (row0 + b, c))
    cache = pl.BlockSpec((None, ATT_PAST, ATT_WIDTH), lambda b: (b, 0, 0))
    return pl.pallas_call(
        _band_sample_kernel,
        grid=(DEC_BATCH,),
        in_specs=[new(0), new(1), new(2), cache, cache,
                  pl.BlockSpec((ATT_HEADS, DEC_SEQ, ATT_PAST), lambda b: (0, 0, 0)),
                  pl.BlockSpec((ATT_HEADS, DEC_SEQ, DEC_SEQ), lambda b: (0, 0, 0))],
        out_specs=pl.BlockSpec((DEC_SEQ, ATT_WIDTH), lambda b: (b, 0)),
        out_shape=jax.ShapeDtypeStruct((DEC_BATCH * DEC_SEQ, ATT_WIDTH), BF16),
        compiler_params=_params(("parallel",)),
        name="band_attention_sample",
    )(qkv, qkv, qkv, cache_k, cache_v, bias_past, bias_new)


def _band_sample_bias(rel_bias):
    q_pos = PAST_LEN + np.arange(DEC_SEQ)[:, None]
    k_pos = PAST_LEN - ATT_PAST + np.arange(ATT_PAST + DEC_SEQ)[None, :]
    rel = np.clip(q_pos - k_pos, -REL_CLIP, REL_CLIP) + REL_CLIP
    qc, kc = q_pos // CHUNK, k_pos // CHUNK
    mask = (k_pos >= 0) & (kc <= qc) & (kc >= qc - N_PREV_CHUNKS)
    table = jnp.take(rel_bias, jnp.asarray(rel.reshape(-1)), axis=1).reshape(
        ATT_HEADS, DEC_SEQ, ATT_PAST + DEC_SEQ)
    table = jnp.where(jnp.asarray(mask)[None], table, NEG_INF).astype(F32)
    return table[:, :, :ATT_PAST], table[:, :, ATT_PAST:]


def _mem_attn_kernel(q_ref, k_ref, v_ref, o_ref):
    s = _dot_nt(q_ref[...], k_ref[...].astype(BF16)) * (MEM_HEAD_DIM ** -0.5)
    m = jnp.max(s, axis=-1, keepdims=True)
    p = jnp.exp(s - m)
    denom = jnp.sum(p, axis=-1, keepdims=True)
    o = jnp.dot(p.astype(BF16), v_ref[...].astype(BF16), preferred_element_type=F32)
    o_ref[...] = (o / denom).astype(o_ref.dtype)


def _mem_attn(q_mem, mem_k, mem_v, *, row0, n_rows, tq, rows_per_mem):
    off = row0 // tq
    kv_spec = pl.BlockSpec((None, N_MEM, MEM_HEAD_DIM), lambda h, i: ((i * tq) // rows_per_mem, 0, h))
    return pl.pallas_call(
        _mem_attn_kernel,
        grid=(MEM_HEADS, n_rows // tq),
        in_specs=[pl.BlockSpec((tq, MEM_HEAD_DIM), lambda h, i: (off + i, h)), kv_spec, kv_spec],
        out_specs=pl.BlockSpec((tq, MEM_HEAD_DIM), lambda h, i: (i, h)),
        out_shape=jax.ShapeDtypeStruct((n_rows, MEM_WIDTH), BF16),
        compiler_params=_params(("parallel", "parallel")),
        name="memory_attention",
    )(q_mem, mem_k, mem_v)


ROUTER_COLS = 128


def _norm_router_kernel(x_ref, g_ref, wr_ref, br_ref, h_ref, id_ref, wt_ref):
    x = x_ref[...]
    ms = jnp.mean(x * x, axis=-1, keepdims=True)
    h = x * lax.rsqrt(ms + NORM_EPS) * g_ref[...]
    bits = lax.bitcast_convert_type(h.astype(BF16).astype(F32), jnp.uint32)
    half = bits.shape[1] // 2
    h_ref[...] = (bits[:, :half] >> 16) | bits[:, half:]
    logits = jnp.dot(h, wr_ref[...], precision=HIGHEST, preferred_element_type=F32) + br_ref[...]

    lane = lax.broadcasted_iota(jnp.int32, logits.shape, 1).astype(F32)
    row_max = lambda a: jnp.max(a, axis=-1, keepdims=True)
    first_at = lambda a, top: jnp.min(jnp.where(a == top, lane, float(ROUTER_COLS)), axis=-1, keepdims=True)
    is_group = lane < N_GROUPS
    g_logit = jnp.where(is_group, logits, NEG_INF)
    g_top = row_max(g_logit)
    g_idx = first_at(g_logit, g_top)
    p_group = 1.0 / jnp.sum(jnp.where(is_group, jnp.exp(logits - g_top), 0.0), axis=-1, keepdims=True)
    lo = N_GROUPS + g_idx * EXPERTS_PER_GROUP
    e_logit = jnp.where((lane >= lo) & (lane < lo + EXPERTS_PER_GROUP), logits, NEG_INF)
    e1 = row_max(e_logit)
    i1 = first_at(e_logit, e1)
    e_rest = jnp.where(lane == i1, NEG_INF, e_logit)
    e2 = row_max(e_rest)
    i2 = first_at(e_rest, e2)
    t = jnp.exp(e2 - e1)
    w1 = p_group / (1.0 + t)
    w2 = p_group * t / (1.0 + t)
    id_ref[...] = jnp.where(lane == 0, i1 - N_GROUPS, jnp.where(lane == 1, i2 - N_GROUPS, 0.0)).astype(jnp.int32)
    wt_ref[...] = jnp.where(lane == 0, w1, jnp.where(lane == 1, w2, 0.0))


def _norm_router(x, g, w_router, b_router, tm):
    m, d = x.shape
    return pl.pallas_call(
        _norm_router_kernel,
        grid=(m // tm,),
        in_specs=[pl.BlockSpec((tm, d), lambda i: (i, 0)),
                  pl.BlockSpec((1, d), lambda i: (0, 0)),
                  pl.BlockSpec((d, ROUTER_COLS), lambda i: (0, 0)),
                  pl.BlockSpec((1, ROUTER_COLS), lambda i: (0, 0))],
        out_specs=[pl.BlockSpec((tm, d // 2), lambda i: (i, 0)),
                   pl.BlockSpec((tm, ROUTER_COLS), lambda i: (i, 0)),
                   pl.BlockSpec((tm, ROUTER_COLS), lambda i: (i, 0))],
        out_shape=[jax.ShapeDtypeStruct((m, d // 2), jnp.uint32),
                   jax.ShapeDtypeStruct((m, ROUTER_COLS), jnp.int32),
                   jax.ShapeDtypeStruct((m, ROUTER_COLS), F32)],
        compiler_params=_params(("parallel",)),
        name="ffn_norm_router",
    )(x, g.reshape(1, d), w_router, b_router)


GATHER_WINDOW = 256


def _gather_rows_kernel(n_valid_ref, idx_ref, src_ref, init_ref, dst_ref, sem, ring_ref, count_ref):
    del init_ref
    b = pl.program_id(0)
    n = n_valid_ref[b]
    base = b * MOE_ROWS

    @pl.when(b == 0)
    def _():
        count_ref[0] = 0

    issued = count_ref[0]

    def row_copy(src_row, dst_row, slot):
        return pltpu.make_async_copy(src_ref.at[pl.ds(src_row, 1)], dst_ref.at[pl.ds(dst_row, 1)],
                                     sem.at[slot])

    def wait_slot(slot):
        row_copy(ring_ref[0, slot], ring_ref[1, slot], slot).wait()

    def issue(r, carry):
        g = issued + r
        slot = g % GATHER_WINDOW

        @pl.when(g >= GATHER_WINDOW)
        def _():
            wait_slot(slot)

        src_row = idx_ref[0, r]
        ring_ref[0, slot] = src_row
        ring_ref[1, slot] = base + r
        row_copy(src_row, base + r, slot).start()
        return carry

    lax.fori_loop(0, n, issue, 0)
    count_ref[0] = issued + n

    @pl.when(b == pl.num_programs(0) - 1)
    def _():
        def drain(slot, carry):
            wait_slot(slot)
            return carry
        lax.fori_loop(0, jnp.minimum(issued + n, GATHER_WINDOW), drain, 0)


def _gather_rows(src, idx, n_valid):
    nb = idx.shape[0]
    rows, width = nb * MOE_ROWS, src.shape[1]
    any_spec = pl.BlockSpec(memory_space=pl.ANY)
    grid_spec = pltpu.PrefetchScalarGridSpec(
        num_scalar_prefetch=1,
        grid=(nb,),
        in_specs=[pl.BlockSpec((None, 1, MOE_ROWS), lambda b, nv: (b, 0, 0), memory_space=pltpu.SMEM),
                  any_spec, any_spec],
        out_specs=any_spec,
        scratch_shapes=[pltpu.SemaphoreType.DMA((GATHER_WINDOW,)),
                        pltpu.SMEM((2, GATHER_WINDOW), jnp.int32),
                        pltpu.SMEM((1,), jnp.int32)],
    )
    return pl.pallas_call(
        _gather_rows_kernel,
        grid_spec=grid_spec,
        out_shape=jax.ShapeDtypeStruct((rows, width), src.dtype),
        input_output_aliases={3: 0},
        compiler_params=_params(("arbitrary",)),
        name="moe_gather_rows",
    )(n_valid, idx, src, jnp.zeros((rows, width), src.dtype))


def _expert_kernel(blk_e_ref, blk_ok_ref, blk_src_ref, x_ref, wg_ref, wu_ref, wd_ref, o_ref, xb_ref):
    del blk_e_ref, blk_src_ref
    b = pl.program_id(0)
    f = pl.program_id(1)
    ok = blk_ok_ref[b] > 0
    half = D_MODEL // 2

    @pl.when(f == 0)
    def _():
        o_ref[...] = jnp.zeros_like(o_ref)

    @pl.when(ok & (f == 0))
    def _():
        u = x_ref[...]
        xb_ref[:, :half] = lax.bitcast_convert_type(u << 16, F32).astype(BF16)
        xb_ref[:, half:] = lax.bitcast_convert_type(u & jnp.uint32(0xFFFF0000), F32).astype(BF16)

    @pl.when(ok)
    def _():
        x = xb_ref[...]
        gate = jnp.dot(x, wg_ref[...].astype(BF16), preferred_element_type=F32)
        up = jnp.dot(x, wu_ref[...].astype(BF16), preferred_element_type=F32)
        hidden = (gate * _sigmoid(gate) * up).astype(BF16)
        for n in range(D_MODEL // MOE_DOWN_COLS):
            cols = slice(n * MOE_DOWN_COLS, (n + 1) * MOE_DOWN_COLS)
            o_ref[:, cols] += jnp.dot(hidden, wd_ref[:, cols].astype(BF16), preferred_element_type=F32)


def _expert_ffn(xs, blk_e, blk_ok, blk_src, w_gate, w_up, w_down):
    rows = xs.shape[0]
    nb = rows // MOE_ROWS
    nf = D_EXPERT // MOE_FCHUNK
    fidx = lambda b, f, ok: jnp.where(ok[b] > 0, f, nf - 1)
    grid_spec = pltpu.PrefetchScalarGridSpec(
        num_scalar_prefetch=3,
        grid=(nb, nf),
        in_specs=[
            pl.BlockSpec((MOE_ROWS, D_MODEL // 2), lambda b, f, e, ok, src: (src[b], 0)),
            pl.BlockSpec((None, D_MODEL, MOE_FCHUNK), lambda b, f, e, ok, src: (e[b], 0, fidx(b, f, ok))),
            pl.BlockSpec((None, D_MODEL, MOE_FCHUNK), lambda b, f, e, ok, src: (e[b], 0, fidx(b, f, ok))),
            pl.BlockSpec((None, MOE_FCHUNK, D_MODEL), lambda b, f, e, ok, src: (e[b], fidx(b, f, ok), 0)),
        ],
        out_specs=pl.BlockSpec((MOE_ROWS, D_MODEL), lambda b, f, e, ok, src: (b, 0)),
        scratch_shapes=[pltpu.VMEM((MOE_ROWS, D_MODEL), BF16)],
    )
    return pl.pallas_call(
        _expert_kernel,
        grid_spec=grid_spec,
        out_shape=jax.ShapeDtypeStruct((rows, D_MODEL), F32),
        compiler_params=_params(("arbitrary", "arbitrary"), MOE_VMEM_LIMIT),
        name="expert_ffn",
    )(blk_e, blk_ok, blk_src, xs, w_gate, w_up, w_down)


def _moe_layout(expert_id):
    n = expert_id.shape[0]
    a = n * TOP_K
    n_blocks = a // MOE_ROWS + N_EXPERTS
    rows = n_blocks * MOE_ROWS
    flat_e = expert_id.reshape(a)
    flat_tok = jnp.arange(a, dtype=jnp.int32) // TOP_K
    order = jnp.argsort(flat_e)
    sorted_e = flat_e[order]
    counts = jnp.bincount(flat_e, length=N_EXPERTS).astype(jnp.int32)
    start = jnp.cumsum(counts) - counts
    padded = (counts + MOE_ROWS - 1) // MOE_ROWS * MOE_ROWS
    pad_end = jnp.cumsum(padded)
    pad_start = pad_end - padded
    dest = (pad_start[sorted_e] + (jnp.arange(a, dtype=jnp.int32) - start[sorted_e])).astype(jnp.int32)
    row_tok = jnp.zeros((rows,), jnp.int32).at[dest].set(flat_tok[order])
    pos = jnp.zeros((a,), jnp.int32).at[order].set(dest).reshape(n, TOP_K)
    blk_start = jnp.arange(n_blocks, dtype=jnp.int32) * MOE_ROWS
    blk_ok = (blk_start < pad_end[-1]).astype(jnp.int32)
    n_ok = jnp.sum(blk_ok)
    blk_e = jnp.minimum(jnp.searchsorted(pad_end, blk_start, side='right'), N_EXPERTS - 1).astype(jnp.int32)
    blk_e = jnp.where(blk_ok > 0, blk_e, blk_e[n_ok - 1])
    blk_src = jnp.where(blk_ok > 0, jnp.arange(n_blocks, dtype=jnp.int32), n_ok - 1).astype(jnp.int32)
    n_valid = jnp.clip(counts[blk_e] - (blk_start - pad_start[blk_e]), 0, MOE_ROWS) * blk_ok
    return row_tok.reshape(n_blocks, 1, MOE_ROWS), n_valid.astype(jnp.int32), pos, blk_e, blk_ok, blk_src


def kernel(x_prompt, x_sample, mem_prompt, cache_att_k, cache_att_v, cache_mem_k, cache_mem_v, state_rwkv, state_rwkv_shift, norm_mix_g, w_in, rwkv_mu, rwkv_w0, rwkv_w2, rwkv_a0, rwkv_a2, rwkv_g2, rwkv_k_k, rwkv_k_a, rwkv_r_k, rwkv_ln_w, rwkv_ln_b, w_rwkv_out, att_q_g, att_k_g, att_rel_bias, w_att_out, mem_norm_g, w_mem_kv, mem_q_g, mem_k_g, w_mem_out, w_o, norm_ffn_g, w_router_group, b_router_group, w_router_expert, b_router_expert, w_exp_gate, w_exp_up, w_exp_down):
    lp = dict(rwkv_mu=rwkv_mu[0], rwkv_w0=rwkv_w0[0], rwkv_w2=rwkv_w2[0], rwkv_a0=rwkv_a0[0],
              rwkv_a2=rwkv_a2[0], rwkv_g2=rwkv_g2[0], rwkv_k_k=rwkv_k_k[0], rwkv_k_a=rwkv_k_a[0],
              rwkv_r_k=rwkv_r_k[0], rwkv_ln_w=rwkv_ln_w[0], rwkv_ln_b=rwkv_ln_b[0])
    n_samp = DEC_BATCH * DEC_SEQ
    x = jnp.concatenate([x_prompt.reshape(SEQ, D_MODEL), x_sample.reshape(n_samp, D_MODEL)], axis=0)

    h = _rmsnorm(x, norm_mix_g[0], 264, BF16)
    w_in0 = w_in[0]
    c1 = RW_COLS
    c2 = c1 + 3 * ATT_WIDTH
    c3 = c2 + MEM_WIDTH
    w_rw = jnp.pad(w_in0[:, :c1].astype(BF16), ((0, 0), (0, RW_COLS_PAD - RW_COLS)))
    z_rw = _matmul(_mm_plain_kernel, h, w_rw, (), (), tm=1056, tn=512, out_dtype=F32, name="proj_rwkv")

    qk_gain = jnp.concatenate([jnp.tile(att_q_g[0], ATT_HEADS), jnp.tile(att_k_g[0], ATT_HEADS),
                               jnp.ones((ATT_WIDTH,), F32)]).reshape(1, 3 * ATT_WIDTH)
    qkv = _matmul(functools.partial(_mm_headnorm_kernel, head=ATT_HEAD_DIM, n_norm_tiles=8),
                  h, w_in0[:, c1:c2].astype(BF16), (qk_gain,),
                  (pl.BlockSpec((1, 512), lambda j, i: (0, j)),),
                  tm=1056, tn=512, out_dtype=F32, name="proj_qkv")
    q_mem = _matmul(functools.partial(_mm_headnorm_kernel, head=MEM_HEAD_DIM, n_norm_tiles=MEM_HEADS),
                    h, w_in0[:, c2:c3].astype(BF16), (jnp.tile(mem_q_g[0], MEM_HEADS).reshape(1, MEM_WIDTH),),
                    (pl.BlockSpec((1, 512), lambda j, i: (0, j)),),
                    tm=1056, tn=512, out_dtype=BF16, name="proj_mem_q")
    gate = _matmul(_mm_sigmoid_kernel, h, w_in0[:, c3:].astype(BF16), (), (),
                   tm=1056, tn=512, out_dtype=BF16, name="proj_gate")

    shift_p = jnp.zeros((1, 1, RW_COLS), F32)
    prep_p = _rwkv_prep(z_rw, shift_p, lp, row0=0, n_seq=1, seq_len=SEQ, tt=128)
    prep_s = _rwkv_prep(z_rw, state_rwkv_shift[0], lp, row0=SEQ, n_seq=DEC_BATCH, seq_len=DEC_SEQ,
                        tt=DEC_SEQ)
    s0_p = jnp.zeros((1, RW_HEADS, RW_HEAD_DIM, RW_HEAD_DIM), F32)
    rw_parts = []
    states = []
    for (r, w, k, v, kk, b, g), s0, n_seq, seq_len, tt in (
            (prep_p, s0_p, 1, SEQ, 128), (prep_s, state_rwkv[0], DEC_BATCH, DEC_SEQ, DEC_SEQ)):
        y, s_new = _rwkv_scan(r, w, k, v, kk, b, jnp.swapaxes(s0, -1, -2),
                              n_seq=n_seq, seq_len=seq_len, tt=tt)
        rw_parts.append(_rwkv_post(y, r, k, v, g, lp, 256))
        states.append(jnp.swapaxes(s_new, -1, -2))
    a_rw = jnp.concatenate(rw_parts, axis=0)

    bias_p = _band_prompt_bias(att_rel_bias[0])
    bias_s1, bias_s2 = _band_sample_bias(att_rel_bias[0])
    att_p = _band_prompt(qkv, bias_p)
    att_s = _band_sample(qkv, cache_att_k[0].reshape(DEC_BATCH, ATT_PAST, ATT_WIDTH),
                         cache_att_v[0].reshape(DEC_BATCH, ATT_PAST, ATT_WIDTH), bias_s1, bias_s2)
    a_att = jnp.concatenate([att_p, att_s], axis=0)

    mem_h = _rmsnorm(mem_prompt.reshape(N_MEM, D_MODEL), mem_norm_g[0], N_MEM, BF16)
    w_kv = w_mem_kv[0]
    mk_p = _matmul(functools.partial(_mm_headnorm_kernel, head=MEM_HEAD_DIM, n_norm_tiles=MEM_HEADS),
                   mem_h, w_kv[:, :MEM_WIDTH].astype(BF16),
                   (jnp.tile(mem_k_g[0], MEM_HEADS).reshape(1, MEM_WIDTH),),
                   (pl.BlockSpec((1, 512), lambda j, i: (0, j)),),
                   tm=N_MEM, tn=512, out_dtype=F32, name="mem_k")
    mv_p = _matmul(_mm_plain_kernel, mem_h, w_kv[:, MEM_WIDTH:].astype(BF16), (), (),
                   tm=N_MEM, tn=512, out_dtype=F32, name="mem_v")
    mem_p = _mem_attn(q_mem, mk_p.reshape(1, N_MEM, MEM_WIDTH), mv_p.reshape(1, N_MEM, MEM_WIDTH),
                      row0=0, n_rows=SEQ, tq=512, rows_per_mem=SEQ)
    mem_s = _mem_attn(q_mem, cache_mem_k[0].reshape(DEC_BATCH, N_MEM, MEM_WIDTH),
                      cache_mem_v[0].reshape(DEC_BATCH, N_MEM, MEM_WIDTH),
                      row0=SEQ, n_rows=n_samp, tq=DEC_SEQ, rows_per_mem=DEC_SEQ)
    a_mem = jnp.concatenate([mem_p, mem_s], axis=0)

    merged = _gated_merge(a_rw, a_att, a_mem, w_rwkv_out[0].astype(BF16), w_att_out[0].astype(BF16),
                          w_mem_out[0].astype(BF16), gate, 528, 512)
    x1 = _matmul(_mm_residual_kernel, merged, w_o[0].astype(BF16), (x,),
                 (pl.BlockSpec((1056, 512), lambda j, i: (i, j)),),
                 tm=1056, tn=512, out_dtype=F32, name="proj_out")

    w_router = jnp.zeros((D_MODEL, ROUTER_COLS), F32)
    w_router = w_router.at[:, :N_GROUPS].set(w_router_group[0])
    w_router = w_router.at[:, N_GROUPS:N_GROUPS + N_EXPERTS].set(w_router_expert[0])
    b_router = jnp.zeros((1, ROUTER_COLS), F32)
    b_router = b_router.at[0, :N_GROUPS].set(b_router_group[0])
    b_router = b_router.at[0, N_GROUPS:N_GROUPS + N_EXPERTS].set(b_router_expert[0])
    h2, route_id, route_w = _norm_router(x1, norm_ffn_g[0], w_router, b_router, 264)
    row_tok, n_valid, pos, blk_e, blk_ok, blk_src = _moe_layout(route_id[:, :TOP_K])
    yb = _expert_ffn(_gather_rows(h2, row_tok, n_valid), blk_e, blk_ok, blk_src,
                     w_exp_gate[0], w_exp_up[0], w_exp_down[0])
    take_rows = lambda idx: jnp.take(yb, idx, axis=0, mode="clip")

    def combine(rows):
        return x1[rows] + (route_w[rows, 0:1] * take_rows(pos[rows, 0])
                           + route_w[rows, 1:2] * take_rows(pos[rows, 1]))

    y_prompt = combine(slice(0, SEQ))
    y_sample = combine(slice(SEQ, N_TOK))

    heads = lambda t, bsz, rows: t.reshape(1, bsz, rows, ATT_HEADS, ATT_HEAD_DIM)
    k_cols = slice(ATT_WIDTH, 2 * ATT_WIDTH)
    v_cols = slice(2 * ATT_WIDTH, 3 * ATT_WIDTH)
    keep = SEQ - N_PREV_CHUNKS * CHUNK
    z_s = z_rw[SEQ:, :RW_COLS].reshape(DEC_BATCH, DEC_SEQ, RW_COLS)
    return (
        y_prompt.reshape(1, SEQ, D_MODEL),
        y_sample.reshape(DEC_BATCH, DEC_SEQ, D_MODEL),
        heads(qkv[keep:SEQ, k_cols], 1, SEQ - keep),
        heads(qkv[keep:SEQ, v_cols], 1, SEQ - keep),
        states[0].reshape(1, 1, RW_HEADS, RW_HEAD_DIM, RW_HEAD_DIM),
        z_rw[SEQ - 1:SEQ, :RW_COLS].reshape(1, 1, 1, RW_COLS),
        mk_p.reshape(1, 1, N_MEM, MEM_HEADS, MEM_HEAD_DIM),
        mv_p.reshape(1, 1, N_MEM, MEM_HEADS, MEM_HEAD_DIM),
        heads(qkv[SEQ:, k_cols], DEC_BATCH, DEC_SEQ),
        heads(qkv[SEQ:, v_cols], DEC_BATCH, DEC_SEQ),
        states[1].reshape(1, DEC_BATCH, RW_HEADS, RW_HEAD_DIM, RW_HEAD_DIM),
        z_s[:, -1:, :].reshape(1, DEC_BATCH, 1, RW_COLS),
    )
```

```python
import functools
import math

import numpy as np
import jax
import jax.numpy as jnp
from jax import lax
from jax.experimental import pallas as pl
from jax.experimental.pallas import tpu as pltpu

F32 = jnp.float32
BF16 = jnp.bfloat16
HIGHEST = lax.Precision.HIGHEST

D_MODEL = 4096
SEQ = 8192
DEC_BATCH = 16
DEC_SEQ = 16
PAST_LEN = 2048
N_TOK = SEQ + DEC_BATCH * DEC_SEQ
CHUNK = 64
N_PREV_CHUNKS = 8
RW_HEADS = 32
RW_HEAD_DIM = 64
RW_WIDTH = 2048
DECAY_LORA = 64
ICL_LORA = 64
GATE_LORA = 256
RW_LN_EPS = 64e-5
RW_COLS = 3 * RW_WIDTH + DECAY_LORA + ICL_LORA + GATE_LORA
RW_COLS_PAD = 6656
ATT_HEADS = 16
ATT_HEAD_DIM = 128
ATT_WIDTH = 2048
ATT_PAST = 512
REL_CLIP = 256
N_MEM = 256
MEM_HEADS = 4
MEM_HEAD_DIM = 512
MEM_WIDTH = 2048
N_GROUPS = 8
EXPERTS_PER_GROUP = 8
N_EXPERTS = 64
TOP_K = 2
D_EXPERT = 1024
NORM_EPS = 1e-6
NEG_INF = -1e30

LANES = 128
VMEM_LIMIT = 48 * 1024 * 1024
MOE_VMEM_LIMIT = 54 * 1024 * 1024
MOE_ROWS = 512
MOE_FCHUNK = 256
MOE_DOWN_COLS = 1024


def _params(sem, limit=VMEM_LIMIT):
    return pltpu.CompilerParams(dimension_semantics=sem, vmem_limit_bytes=limit)


def _sigmoid(x):
    return 1.0 / (1.0 + jnp.exp(-x))


def _rmsnorm_kernel(x_ref, g_ref, o_ref):
    x = x_ref[...]
    ms = jnp.mean(x * x, axis=-1, keepdims=True)
    o_ref[...] = (x * lax.rsqrt(ms + NORM_EPS) * g_ref[...]).astype(o_ref.dtype)


def _rmsnorm(x, g, tm, out_dtype):
    m, d = x.shape
    return pl.pallas_call(
        _rmsnorm_kernel,
        grid=(m // tm,),
        in_specs=[pl.BlockSpec((tm, d), lambda i: (i, 0)),
                  pl.BlockSpec((1, d), lambda i: (0, 0))],
        out_specs=pl.BlockSpec((tm, d), lambda i: (i, 0)),
        out_shape=jax.ShapeDtypeStruct((m, d), out_dtype),
        compiler_params=_params(("parallel",)),
        name="rmsnorm",
    )(x, g.reshape(1, d))


def _mm_plain_kernel(x_ref, w_ref, o_ref):
    o_ref[...] = jnp.dot(x_ref[...], w_ref[...], preferred_element_type=F32).astype(o_ref.dtype)


def _mm_headnorm_kernel(x_ref, w_ref, g_ref, o_ref, *, head, n_norm_tiles):
    acc = jnp.dot(x_ref[...], w_ref[...], preferred_element_type=F32)
    j = pl.program_id(0)
    tn = acc.shape[1]

    @pl.when(j < n_norm_tiles)
    def _():
        for c in range(tn // head):
            blk = acc[:, c * head:(c + 1) * head]
            ms = jnp.mean(blk * blk, axis=-1, keepdims=True)
            o_ref[:, c * head:(c + 1) * head] = (
                blk * lax.rsqrt(ms + NORM_EPS) * g_ref[:, c * head:(c + 1) * head]).astype(o_ref.dtype)

    @pl.when(j >= n_norm_tiles)
    def _():
        o_ref[...] = acc.astype(o_ref.dtype)


def _mm_sigmoid_kernel(x_ref, w_ref, o_ref):
    acc = jnp.dot(x_ref[...], w_ref[...], preferred_element_type=F32)
    o_ref[...] = _sigmoid(acc).astype(o_ref.dtype)


def _mm_residual_kernel(x_ref, w_ref, r_ref, o_ref):
    o_ref[...] = r_ref[...] + jnp.dot(x_ref[...], w_ref[...], preferred_element_type=F32)


def _matmul(body, x, w, extras, extra_specs, *, tm, tn, out_dtype, name):
    m, k = x.shape
    n = w.shape[1]
    return pl.pallas_call(
        body,
        grid=(n // tn, m // tm),
        in_specs=[pl.BlockSpec((tm, k), lambda j, i: (i, 0)),
                  pl.BlockSpec((k, tn), lambda j, i: (0, j))] + list(extra_specs),
        out_specs=pl.BlockSpec((tm, tn), lambda j, i: (i, j)),
        out_shape=jax.ShapeDtypeStruct((m, n), out_dtype),
        compiler_params=_params(("parallel", "parallel")),
        name=name,
    )(x, w, *extras)


def _merge_kernel(a0_ref, a1_ref, a2_ref, w0_ref, w1_ref, w2_ref, g0_ref, g1_ref, g2_ref, o_ref):
    y0 = jnp.dot(a0_ref[...], w0_ref[...], preferred_element_type=F32)
    y1 = jnp.dot(a1_ref[...], w1_ref[...], preferred_element_type=F32)
    y2 = jnp.dot(a2_ref[...], w2_ref[...], preferred_element_type=F32)
    merged = (g0_ref[...].astype(F32) * y0 + g1_ref[...].astype(F32) * y1
              + g2_ref[...].astype(F32) * y2)
    o_ref[...] = merged.astype(o_ref.dtype)


def _gated_merge(a_rw, a_att, a_mem, w_rw, w_att, w_mem, gate, tm, tn):
    m, k = a_rw.shape
    n = w_rw.shape[1]
    nj = n // tn
    a_spec = pl.BlockSpec((tm, k), lambda j, i: (i, 0))
    w_spec = pl.BlockSpec((k, tn), lambda j, i: (0, j))
    g_specs = [pl.BlockSpec((tm, tn), functools.partial(lambda j, i, b: (i, b * nj + j), b=b))
               for b in range(3)]
    return pl.pallas_call(
        _merge_kernel,
        grid=(nj, m // tm),
        in_specs=[a_spec, a_spec, a_spec, w_spec, w_spec, w_spec] + g_specs,
        out_specs=pl.BlockSpec((tm, tn), lambda j, i: (i, j)),
        out_shape=jax.ShapeDtypeStruct((m, n), BF16),
        compiler_params=_params(("parallel", "parallel")),
        name="gated_merge",
    )(a_rw, a_att, a_mem, w_rw, w_att, w_mem, gate, gate, gate)


def _half_ones():
    qi = lax.broadcasted_iota(jnp.int32, (LANES, LANES), 0) // RW_HEAD_DIM
    qj = lax.broadcasted_iota(jnp.int32, (LANES, LANES), 1) // RW_HEAD_DIM
    return qi == qj


def _rwkv_prep_kernel(z_ref, zp_ref, sh_ref, mu_ref, w0_ref, a0_ref, kk_ref, ka_ref,
                      w2_ref, a2_ref, g2_ref,
                      r_o, w_o, k_o, v_o, kk_o, b_o, g_o, *, tiles_per_seq):
    i = pl.program_id(0)
    first = (i % tiles_per_seq) == 0
    z = z_ref[...]
    prev_row = jnp.where(first, sh_ref[...], zp_ref[7:8, :])
    rows = lax.broadcasted_iota(jnp.int32, z.shape, 0)
    z_prev = jnp.where(rows == 0, prev_row, pltpu.roll(z, 1, 0))
    zs = z + (z_prev - z) * mu_ref[...]

    c1, c2, c3 = RW_WIDTH, 2 * RW_WIDTH, 3 * RW_WIDTH
    c4, c5 = c3 + DECAY_LORA, c3 + DECAY_LORA + ICL_LORA
    r = zs[:, :c1]
    k = zs[:, c1:c2]
    v = zs[:, c2:c3]
    wl = w0_ref[...] + jnp.dot(jnp.tanh(zs[:, c3:c4]), w2_ref[...], precision=HIGHEST,
                               preferred_element_type=F32)
    decay = jnp.exp(-math.exp(-0.5) * _sigmoid(wl))
    a = _sigmoid(a0_ref[...] + jnp.dot(zs[:, c4:c5], a2_ref[...], precision=HIGHEST,
                                       preferred_element_type=F32))
    g = jnp.dot(_sigmoid(zs[:, c5:]), g2_ref[...], precision=HIGHEST, preferred_element_type=F32)

    q = _half_ones().astype(F32)
    kk = k * kk_ref[...]
    for c in range(RW_WIDTH // LANES):
        sl = slice(c * LANES, (c + 1) * LANES)
        blk = kk[:, sl]
        ss = jnp.dot(blk * blk, q, precision=HIGHEST, preferred_element_type=F32)
        blk = blk / jnp.maximum(jnp.sqrt(ss), 1e-12)
        kk_o[:, sl] = blk
        b_o[:, sl] = blk * a[:, sl]
    r_o[...] = r
    w_o[...] = decay
    k_o[...] = k * (1.0 + (a - 1.0) * ka_ref[...])
    v_o[...] = v
    g_o[...] = g


def _rwkv_prep(z_rw, shift, lp, *, row0, n_seq, seq_len, tt):
    tiles_per_seq = seq_len // tt
    n_tiles = n_seq * tiles_per_seq
    off = row0 // tt
    off8 = row0 // 8
    row = lambda a: a.reshape(1, -1)
    full = lambda shape: pl.BlockSpec(shape, lambda i: (0, 0))
    out_sd = jax.ShapeDtypeStruct((n_seq * seq_len, RW_WIDTH), F32)
    o_spec = pl.BlockSpec((tt, RW_WIDTH), lambda i: (i, 0))
    return pl.pallas_call(
        functools.partial(_rwkv_prep_kernel, tiles_per_seq=tiles_per_seq),
        grid=(n_tiles,),
        in_specs=[
            pl.BlockSpec((tt, RW_COLS), lambda i: (off + i, 0)),
            pl.BlockSpec((8, RW_COLS), lambda i: (jnp.maximum(off8 + i * (tt // 8) - 1, 0), 0)),
            pl.BlockSpec((None, 1, RW_COLS), lambda i: (i // tiles_per_seq, 0, 0)),
            full((1, RW_COLS)), full((1, RW_WIDTH)), full((1, RW_WIDTH)), full((1, RW_WIDTH)),
            full((1, RW_WIDTH)), full((DECAY_LORA, RW_WIDTH)), full((ICL_LORA, RW_WIDTH)),
            full((GATE_LORA, RW_WIDTH)),
        ],
        out_specs=[o_spec] * 7,
        out_shape=[out_sd] * 7,
        compiler_params=_params(("parallel",)),
        name="rwkv_prep",
    )(z_rw, z_rw, shift, row(lp['rwkv_mu']), row(lp['rwkv_w0']), row(lp['rwkv_a0']),
      row(lp['rwkv_k_k']), row(lp['rwkv_k_a']), lp['rwkv_w2'], lp['rwkv_a2'], lp['rwkv_g2'])


RW_CHUNK = 8


def _split_bf16(x):
    hi = x.astype(BF16)
    return hi, (x - hi.astype(F32)).astype(BF16)


def _rwkv_scan_kernel(r_ref, w_ref, k_ref, v_ref, kk_ref, b_ref, s0_ref, y_ref, st_ref, s_scr,
                      *, tt, pairs):
    c = RW_CHUNK
    tb = pl.program_id(1)
    n_tb = pl.num_programs(1)
    left64 = lax.broadcasted_iota(jnp.int32, (RW_HEAD_DIM, LANES), 1) < RW_HEAD_DIM
    row_c = lax.broadcasted_iota(jnp.int32, (c, LANES), 0)
    q = _half_ones().astype(BF16)
    q2 = jnp.concatenate([q, q], axis=0)
    row_2c = lax.broadcasted_iota(jnp.int32, (2 * c, LANES), 0)
    ones_rows = jnp.where(row_2c < 2, 1.0, 0.0).astype(BF16)

    def segsum(x):
        hi, lo = _split_bf16(x)
        return jnp.dot(jnp.concatenate([hi, lo], axis=1), q2, preferred_element_type=F32)

    def shift_rows(x, n, fill):
        return jnp.where(row_c >= n, pltpu.roll(x, n, 0), fill)

    def per_head_blocks(x):
        zero = jnp.zeros_like(x)
        return jnp.concatenate([jnp.where(left64, x, zero), jnp.where(left64, zero, x)], axis=0)

    def own_head(x):
        return jnp.where(left64, x[:RW_HEAD_DIM], x[RW_HEAD_DIM:])

    @pl.when(tb == 0)
    def _():
        for p in range(pairs):
            s_scr[p] = jnp.concatenate([s0_ref[0, 2 * p], s0_ref[0, 2 * p + 1]], axis=1)

    zero_2c = jnp.zeros((2 * c, LANES), BF16)

    def chunk(ci, carry):
        rows = pl.ds(pl.multiple_of(ci * c, c), c)
        lanes = [pl.ds(p * LANES, LANES) for p in range(pairs)]
        vs, keys, states, grams, from_state = [], [], [], [], []
        for p in range(pairs):
            r, w, k, v, kk, b = (ref[rows, lanes[p]] for ref in (r_ref, w_ref, k_ref, v_ref, kk_ref, b_ref))
            gam = w
            n = 1
            while n < c:
                gam = gam * shift_rows(gam, n, 1.0)
                n *= 2
            inv = 1.0 / gam
            a_t = -(shift_rows(gam, 1, 1.0) * kk)
            b_t = b * inv
            k_t = k * inv
            r_t = gam * r
            g_end = gam[c - 1:c, :]
            prods = []
            for s in range(c):
                bs, ks = b_t[s:s + 1, :], k_t[s:s + 1, :]
                prods += [a_t * bs, a_t * ks, r_t * bs, r_t * ks]
            grams.append(segsum(jnp.concatenate(prods, axis=0)))
            vs.append(v)
            keys.append((jnp.concatenate([b_t, k_t], axis=0) * g_end, g_end,
                         _split_bf16(jnp.concatenate([a_t, r_t], axis=0))))
        for p in range(pairs):
            st = s_scr[p]
            s_hi, s_lo = _split_bf16(st)
            ar_hi, ar_lo = keys[p][2]
            states.append(st)
            from_state.append(jnp.dot(
                jnp.concatenate([ar_hi, ar_hi, ar_lo], axis=1),
                jnp.concatenate([per_head_blocks(s_hi), per_head_blocks(s_lo), per_head_blocks(s_hi)], axis=0),
                preferred_element_type=F32))
        operands = []
        for p in range(pairs):
            v = vs[p]
            g = lambda s, m: grams[p][(4 * s + m) * c:(4 * s + m + 1) * c, :]
            d = from_state[p][:c]
            y = from_state[p][c:]
            for s in range(c - 1):
                d = d + jnp.where(row_c > s, g(s, 1), 0.0) * v[s:s + 1, :]
            sa_rows = []
            for s in range(c):
                sa = d[s:s + 1, :]
                sa_rows.append(sa)
                if s < c - 1:
                    d = d + g(s, 0) * sa
                y = (y + jnp.where(row_c >= s, g(s, 2), 0.0) * sa
                     + jnp.where(row_c >= s, g(s, 3), 0.0) * v[s:s + 1, :])
            y_ref[rows, lanes[p]] = y
            key, g_end, _ = keys[p]
            key_hi, key_lo = _split_bf16(key)
            val_hi, val_lo = _split_bf16(jnp.concatenate(sa_rows + [v], axis=0))
            g_hi = g_end.astype(BF16).astype(F32)
            g_rows = jnp.where(row_2c == 0, g_hi, jnp.where(row_2c == 1, g_end - g_hi, 0.0)).astype(BF16)
            operands.append((
                jnp.concatenate([key_hi, key_hi, key_lo, g_rows], axis=0),
                jnp.concatenate([
                    jnp.concatenate([val_hi, val_lo, val_hi, zero_2c], axis=0),
                    jnp.concatenate([zero_2c, zero_2c, zero_2c, ones_rows], axis=0)], axis=1)))
        for p in range(pairs):
            out = lax.dot_general(operands[p][0], operands[p][1], (((0,), (0,)), ((), ())),
                                  preferred_element_type=F32)
            s_scr[p] = own_head(out[:, LANES:]) * states[p] + own_head(out[:, :LANES])
        return carry

    lax.fori_loop(0, tt // c, chunk, 0)

    @pl.when(tb == n_tb - 1)
    def _():
        for p in range(pairs):
            st_ref[0, 2 * p] = s_scr[p][:, :RW_HEAD_DIM]
            st_ref[0, 2 * p + 1] = s_scr[p][:, RW_HEAD_DIM:]


def _rwkv_scan(r, w, k, v, kk, b, s0, *, n_seq, seq_len, tt):
    n_tb = seq_len // tt
    pairs = RW_HEADS // 2
    t_spec = pl.BlockSpec((tt, RW_WIDTH), lambda s, t: (s * n_tb + t, 0))
    s_spec = pl.BlockSpec((1, RW_HEADS, RW_HEAD_DIM, RW_HEAD_DIM), lambda s, t: (s, 0, 0, 0))
    return pl.pallas_call(
        functools.partial(_rwkv_scan_kernel, tt=tt, pairs=pairs),
        grid=(n_seq, n_tb),
        in_specs=[t_spec] * 6 + [s_spec],
        out_specs=[t_spec, s_spec],
        out_shape=[jax.ShapeDtypeStruct((n_seq * seq_len, RW_WIDTH), F32),
                   jax.ShapeDtypeStruct((n_seq, RW_HEADS, RW_HEAD_DIM, RW_HEAD_DIM), F32)],
        scratch_shapes=[pltpu.VMEM((pairs, RW_HEAD_DIM, LANES), F32)],
        compiler_params=_params(("parallel", "arbitrary")),
        name="rwkv_scan",
    )(r, w, k, v, kk, b, s0)


def _rwkv_post_kernel(y_ref, r_ref, k_ref, v_ref, g_ref, lnw_ref, lnb_ref, rk_ref, o_ref):
    q = _half_ones().astype(F32)
    seg = lambda x: jnp.dot(x, q, precision=HIGHEST, preferred_element_type=F32)
    for c in range(RW_WIDTH // LANES):
        sl = slice(c * LANES, (c + 1) * LANES)
        y = y_ref[:, sl]
        mean = seg(y) * (1.0 / RW_HEAD_DIM)
        d = y - mean
        var = seg(d * d) * (1.0 / RW_HEAD_DIM)
        yn = d * lax.rsqrt(var + RW_LN_EPS) * lnw_ref[:, sl] + lnb_ref[:, sl]
        bonus = seg(r_ref[:, sl] * k_ref[:, sl] * rk_ref[:, sl]) * v_ref[:, sl]
        o_ref[:, sl] = ((yn + bonus) * g_ref[:, sl]).astype(o_ref.dtype)


def _rwkv_post(y, r, k, v, g, lp, tt):
    m = y.shape[0]
    row = lambda a: a.reshape(1, -1)
    t_spec = pl.BlockSpec((tt, RW_WIDTH), lambda i: (i, 0))
    p_spec = pl.BlockSpec((1, RW_WIDTH), lambda i: (0, 0))
    return pl.pallas_call(
        _rwkv_post_kernel,
        grid=(m // tt,),
        in_specs=[t_spec] * 5 + [p_spec] * 3,
        out_specs=t_spec,
        out_shape=jax.ShapeDtypeStruct((m, RW_WIDTH), BF16),
        compiler_params=_params(("parallel",)),
        name="rwkv_post",
    )(y, r, k, v, g, row(lp['rwkv_ln_w']), row(lp['rwkv_ln_b']), row(lp['rwkv_r_k']))


BAND_TQ = 512


def _softmax2(s1, s2):
    m = jnp.maximum(jnp.max(s1, axis=-1, keepdims=True), jnp.max(s2, axis=-1, keepdims=True))
    p1 = jnp.exp(s1 - m)
    p2 = jnp.exp(s2 - m)
    denom = jnp.sum(p1, axis=-1, keepdims=True) + jnp.sum(p2, axis=-1, keepdims=True)
    return p1, p2, denom


def _dot_nt(a, b):
    return lax.dot_general(a, b, (((1,), (1,)), ((), ())), preferred_element_type=F32)


def _band_prompt_kernel(q_ref, kp_ref, kc_ref, vp_ref, vc_ref, bias_ref, o_ref):
    i = pl.program_id(1)
    scale = ATT_HEAD_DIM ** -0.5
    q = q_ref[...].astype(BF16)
    s1 = _dot_nt(q, kp_ref[...].astype(BF16)) * scale + bias_ref[:, :BAND_TQ]
    s2 = _dot_nt(q, kc_ref[...].astype(BF16)) * scale + bias_ref[:, BAND_TQ:]
    s1 = jnp.where(i == 0, NEG_INF, s1)
    p1, p2, denom = _softmax2(s1, s2)
    o = (jnp.dot(p1.astype(BF16), vp_ref[...].astype(BF16), preferred_element_type=F32)
         + jnp.dot(p2.astype(BF16), vc_ref[...].astype(BF16), preferred_element_type=F32))
    o_ref[...] = (o / denom).astype(o_ref.dtype)


def _band_prompt(qkv, bias):
    nq = SEQ // BAND_TQ
    prev = lambda h, i: jnp.maximum(i - 1, 0)
    blk = (BAND_TQ, ATT_HEAD_DIM)
    return pl.pallas_call(
        _band_prompt_kernel,
        grid=(ATT_HEADS, nq),
        in_specs=[
            pl.BlockSpec(blk, lambda h, i: (i, h)),
            pl.BlockSpec(blk, lambda h, i: (prev(h, i), ATT_HEADS + h)),
            pl.BlockSpec(blk, lambda h, i: (i, ATT_HEADS + h)),
            pl.BlockSpec(blk, lambda h, i: (prev(h, i), 2 * ATT_HEADS + h)),
            pl.BlockSpec(blk, lambda h, i: (i, 2 * ATT_HEADS + h)),
            pl.BlockSpec((None, BAND_TQ, 2 * BAND_TQ), lambda h, i: (h, 0, 0)),
        ],
        out_specs=pl.BlockSpec(blk, lambda h, i: (i, h)),
        out_shape=jax.ShapeDtypeStruct((SEQ, ATT_WIDTH), BF16),
        compiler_params=_params(("parallel", "parallel")),
        name="band_attention_prompt",
    )(qkv, qkv, qkv, qkv, qkv, bias)


def _band_prompt_bias(rel_bias):
    iq = np.arange(BAND_TQ)[:, None]
    ik = np.arange(2 * BAND_TQ)[None, :]
    period = 3 * BAND_TQ
    m = np.arange(period)
    k_minus_q = np.where(m <= 2 * BAND_TQ, m, m - period)
    idx = np.clip(BAND_TQ - k_minus_q, -REL_CLIP, REL_CLIP) + REL_CLIP
    vec = jnp.take(rel_bias, jnp.asarray(idx), axis=1)
    flat = jnp.broadcast_to(vec[:, None, :], (ATT_HEADS, BAND_TQ, period)).reshape(ATT_HEADS, -1)
    table = flat[:, :BAND_TQ * (period - 1)].reshape(ATT_HEADS, BAND_TQ, period - 1)[:, :, :2 * BAND_TQ]
    qc = iq // CHUNK + N_PREV_CHUNKS
    kc = ik // CHUNK
    mask = (kc <= qc) & (kc >= qc - N_PREV_CHUNKS)
    return jnp.where(jnp.asarray(mask)[None], table, NEG_INF).astype(F32)


def _band_sample_kernel(q_ref, k_ref, v_ref, kc_ref, vc_ref, b1_ref, b2_ref, o_ref):
    scale = ATT_HEAD_DIM ** -0.5
    for h in range(ATT_HEADS):
        sl = slice(h * ATT_HEAD_DIM, (h + 1) * ATT_HEAD_DIM)
        q = q_ref[:, sl].astype(BF16)
        s1 = _dot_nt(q, kc_ref[:, sl].astype(BF16)) * scale + b1_ref[h]
        s2 = _dot_nt(q, k_ref[:, sl].astype(BF16)) * scale + b2_ref[h]
        p1, p2, denom = _softmax2(s1, s2)
        o = (jnp.dot(p1.astype(BF16), vc_ref[:, sl].astype(BF16), preferred_element_type=F32)
             + jnp.dot(p2.astype(BF16), v_ref[:, sl].astype(BF16), preferred_element_type=F32))
        o_ref[:, sl] = (o / denom).astype(o_ref.dtype)


def _band_sample(qkv, cache_k, cache_v, bias_past, bias_new):
    row0 = SEQ // DEC_SEQ
    new = lambda c: pl.BlockSpec((DEC_SEQ, ATT_WIDTH), lambda b: (row0 + b, c))
    cache = pl.BlockSpec((None, ATT_PAST, ATT_WIDTH), lambda b: (b, 0, 0))
    return pl.pallas_call(
        _band_sample_kernel,
        grid=(DEC_BATCH,),
        in_specs=[new(0), new(1), new(2), cache, cache,
                  pl.BlockSpec((ATT_HEADS, DEC_SEQ, ATT_PAST), lambda b: (0, 0, 0)),
                  pl.BlockSpec((ATT_HEADS, DEC_SEQ, DEC_SEQ), lambda b: (0, 0, 0))],
        out_specs=pl.BlockSpec((DEC_SEQ, ATT_WIDTH), lambda b: (b, 0)),
        out_shape=jax.ShapeDtypeStruct((DEC_BATCH * DEC_SEQ, ATT_WIDTH), BF16),
        compiler_params=_params(("parallel",)),
        name="band_attention_sample",
    )(qkv, qkv, qkv, cache_k, cache_v, bias_past, bias_new)


def _band_sample_bias(rel_bias):
    q_pos = PAST_LEN + np.arange(DEC_SEQ)[:, None]
    k_pos = PAST_LEN - ATT_PAST + np.arange(ATT_PAST + DEC_SEQ)[None, :]
    rel = np.clip(q_pos - k_pos, -REL_CLIP, REL_CLIP) + REL_CLIP
    qc, kc = q_pos // CHUNK, k_pos // CHUNK
    mask = (k_pos >= 0) & (kc <= qc) & (kc >= qc - N_PREV_CHUNKS)
    table = jnp.take(rel_bias, jnp.asarray(rel.reshape(-1)), axis=1).reshape(
        ATT_HEADS, DEC_SEQ, ATT_PAST + DEC_SEQ)
    table = jnp.where(jnp.asarray(mask)[None], table, NEG_INF).astype(F32)
    return table[:, :, :ATT_PAST], table[:, :, ATT_PAST:]


def _mem_attn_kernel(q_ref, k_ref, v_ref, o_ref):
    s = _dot_nt(q_ref[...], k_ref[...].astype(BF16)) * (MEM_HEAD_DIM ** -0.5)
    m = jnp.max(s, axis=-1, keepdims=True)
    p = jnp.exp(s - m)
    denom = jnp.sum(p, axis=-1, keepdims=True)
    o = jnp.dot(p.astype(BF16), v_ref[...].astype(BF16), preferred_element_type=F32)
    o_ref[...] = (o / denom).astype(o_ref.dtype)


def _mem_attn(q_mem, mem_k, mem_v, *, row0, n_rows, tq, rows_per_mem):
    off = row0 // tq
    kv_spec = pl.BlockSpec((None, N_MEM, MEM_HEAD_DIM), lambda h, i: ((i * tq) // rows_per_mem, 0, h))
    return pl.pallas_call(
        _mem_attn_kernel,
        grid=(MEM_HEADS, n_rows // tq),
        in_specs=[pl.BlockSpec((tq, MEM_HEAD_DIM), lambda h, i: (off + i, h)), kv_spec, kv_spec],
        out_specs=pl.BlockSpec((tq, MEM_HEAD_DIM), lambda h, i: (i, h)),
        out_shape=jax.ShapeDtypeStruct((n_rows, MEM_WIDTH), BF16),
        compiler_params=_params(("parallel", "parallel")),
        name="memory_attention",
    )(q_mem, mem_k, mem_v)


ROUTER_COLS = 128


def _norm_router_kernel(x_ref, g_ref, wr_ref, br_ref, h_ref, id_ref, wt_ref):
    x = x_ref[...]
    ms = jnp.mean(x * x, axis=-1, keepdims=True)
    h = x * lax.rsqrt(ms + NORM_EPS) * g_ref[...]
    h_ref[...] = h.astype(h_ref.dtype)
    logits = jnp.dot(h, wr_ref[...], precision=HIGHEST, preferred_element_type=F32) + br_ref[...]

    lane = lax.broadcasted_iota(jnp.int32, logits.shape, 1).astype(F32)
    row_max = lambda a: jnp.max(a, axis=-1, keepdims=True)
    first_at = lambda a, top: jnp.min(jnp.where(a == top, lane, float(ROUTER_COLS)), axis=-1, keepdims=True)
    is_group = lane < N_GROUPS
    g_logit = jnp.where(is_group, logits, NEG_INF)
    g_top = row_max(g_logit)
    g_idx = first_at(g_logit, g_top)
    p_group = 1.0 / jnp.sum(jnp.where(is_group, jnp.exp(logits - g_top), 0.0), axis=-1, keepdims=True)
    lo = N_GROUPS + g_idx * EXPERTS_PER_GROUP
    e_logit = jnp.where((lane >= lo) & (lane < lo + EXPERTS_PER_GROUP), logits, NEG_INF)
    e1 = row_max(e_logit)
    i1 = first_at(e_logit, e1)
    e_rest = jnp.where(lane == i1, NEG_INF, e_logit)
    e2 = row_max(e_rest)
    i2 = first_at(e_rest, e2)
    t = jnp.exp(e2 - e1)
    w1 = p_group / (1.0 + t)
    w2 = p_group * t / (1.0 + t)
    id_ref[...] = jnp.where(lane == 0, i1 - N_GROUPS, jnp.where(lane == 1, i2 - N_GROUPS, 0.0)).astype(jnp.int32)
    wt_ref[...] = jnp.where(lane == 0, w1, jnp.where(lane == 1, w2, 0.0))


def _norm_router(x, g, w_router, b_router, tm):
    m, d = x.shape
    return pl.pallas_call(
        _norm_router_kernel,
        grid=(m // tm,),
        in_specs=[pl.BlockSpec((tm, d), lambda i: (i, 0)),
                  pl.BlockSpec((1, d), lambda i: (0, 0)),
                  pl.BlockSpec((d, ROUTER_COLS), lambda i: (0, 0)),
                  pl.BlockSpec((1, ROUTER_COLS), lambda i: (0, 0))],
        out_specs=[pl.BlockSpec((tm, d), lambda i: (i, 0)),
                   pl.BlockSpec((tm, ROUTER_COLS), lambda i: (i, 0)),
                   pl.BlockSpec((tm, ROUTER_COLS), lambda i: (i, 0))],
        out_shape=[jax.ShapeDtypeStruct((m, d), BF16),
                   jax.ShapeDtypeStruct((m, ROUTER_COLS), jnp.int32),
                   jax.ShapeDtypeStruct((m, ROUTER_COLS), F32)],
        compiler_params=_params(("parallel",)),
        name="ffn_norm_router",
    )(x, g.reshape(1, d), w_router, b_router)


MOE_SEG_ALIGN = 16


def _expert_kernel(blk_e_ref, blk_ok_ref, blk_x0_ref, xs_ref, wg_ref, wu_ref, wd_ref, o_ref, xbuf, sem):
    del blk_e_ref
    b = pl.program_id(0)
    f = pl.program_id(1)
    nb = pl.num_programs(0)
    ok = blk_ok_ref[b] > 0
    slot = b % 2

    def x_copy(blk, dst_slot):
        start = pl.multiple_of(blk_x0_ref[blk], MOE_SEG_ALIGN)
        return pltpu.make_async_copy(xs_ref.at[pl.ds(start, MOE_ROWS)], xbuf.at[dst_slot], sem.at[dst_slot])

    @pl.when(f == 0)
    def _():
        o_ref[...] = jnp.zeros_like(o_ref)

    @pl.when(ok & (f == 0))
    def _():
        @pl.when(b == 0)
        def _():
            x_copy(0, 0).start()

        x_copy(b, slot).wait()
        nxt = jnp.minimum(b + 1, nb - 1)

        @pl.when((b + 1 < nb) & (blk_ok_ref[nxt] > 0))
        def _():
            x_copy(nxt, 1 - slot).start()

    @pl.when(ok)
    def _():
        x = xbuf[slot]
        gate = jnp.dot(x, wg_ref[...].astype(BF16), preferred_element_type=F32)
        up = jnp.dot(x, wu_ref[...].astype(BF16), preferred_element_type=F32)
        hidden = (gate * _sigmoid(gate) * up).astype(BF16)
        for n in range(D_MODEL // MOE_DOWN_COLS):
            cols = slice(n * MOE_DOWN_COLS, (n + 1) * MOE_DOWN_COLS)
            o_ref[:, cols] += jnp.dot(hidden, wd_ref[:, cols].astype(BF16), preferred_element_type=F32)


def _expert_ffn(xs, blk_e, blk_ok, blk_x0, w_gate, w_up, w_down):
    nb = blk_e.shape[0]
    nf = D_EXPERT // MOE_FCHUNK
    fidx = lambda b, f, ok: jnp.where(ok[b] > 0, f, nf - 1)
    grid_spec = pltpu.PrefetchScalarGridSpec(
        num_scalar_prefetch=3,
        grid=(nb, nf),
        in_specs=[
            pl.BlockSpec(memory_space=pl.ANY),
            pl.BlockSpec((None, D_MODEL, MOE_FCHUNK), lambda b, f, e, ok, x0: (e[b], 0, fidx(b, f, ok))),
            pl.BlockSpec((None, D_MODEL, MOE_FCHUNK), lambda b, f, e, ok, x0: (e[b], 0, fidx(b, f, ok))),
            pl.BlockSpec((None, MOE_FCHUNK, D_MODEL), lambda b, f, e, ok, x0: (e[b], fidx(b, f, ok), 0)),
        ],
        out_specs=pl.BlockSpec((MOE_ROWS, D_MODEL), lambda b, f, e, ok, x0: (b, 0),
                               pipeline_mode=pl.Buffered(1)),
        scratch_shapes=[pltpu.VMEM((2, MOE_ROWS, D_MODEL), BF16), pltpu.SemaphoreType.DMA((2,))],
    )
    return pl.pallas_call(
        _expert_kernel,
        grid_spec=grid_spec,
        out_shape=jax.ShapeDtypeStruct((nb * MOE_ROWS, D_MODEL), F32),
        compiler_params=_params(("arbitrary", "arbitrary"), MOE_VMEM_LIMIT),
        name="expert_ffn",
    )(blk_e, blk_ok, blk_x0, xs, w_gate, w_up, w_down)


def _moe_layout(expert_id):
    n = expert_id.shape[0]
    a = n * TOP_K
    n_blocks = a // MOE_ROWS + N_EXPERTS
    in_rows = a + N_EXPERTS * MOE_SEG_ALIGN + MOE_ROWS
    flat_e = expert_id.reshape(a)
    flat_tok = jnp.arange(a, dtype=jnp.int32) // TOP_K
    order = jnp.argsort(flat_e)
    sorted_e = flat_e[order]
    counts = jnp.bincount(flat_e, length=N_EXPERTS).astype(jnp.int32)
    start = jnp.cumsum(counts) - counts
    rank = jnp.arange(a, dtype=jnp.int32) - start[sorted_e]
    seg = (counts + MOE_SEG_ALIGN - 1) // MOE_SEG_ALIGN * MOE_SEG_ALIGN
    seg_start = jnp.cumsum(seg) - seg
    padded = (counts + MOE_ROWS - 1) // MOE_ROWS * MOE_ROWS
    pad_end = jnp.cumsum(padded)
    pad_start = pad_end - padded
    in_tok = jnp.zeros((in_rows,), jnp.int32).at[seg_start[sorted_e] + rank].set(flat_tok[order])
    pos = jnp.zeros((a,), jnp.int32).at[order].set(pad_start[sorted_e] + rank).reshape(n, TOP_K)
    blk_start = jnp.arange(n_blocks, dtype=jnp.int32) * MOE_ROWS
    blk_ok = (blk_start < pad_end[-1]).astype(jnp.int32)
    n_ok = jnp.sum(blk_ok)
    blk_e = jnp.minimum(jnp.searchsorted(pad_end, blk_start, side='right'), N_EXPERTS - 1).astype(jnp.int32)
    blk_e = jnp.where(blk_ok > 0, blk_e, blk_e[n_ok - 1])
    blk_x0 = jnp.where(blk_ok > 0, seg_start[blk_e] + (blk_start - pad_start[blk_e]), 0).astype(jnp.int32)
    return in_tok, pos, blk_e, blk_ok, blk_x0


def kernel(x_prompt, x_sample, mem_prompt, cache_att_k, cache_att_v, cache_mem_k, cache_mem_v, state_rwkv, state_rwkv_shift, norm_mix_g, w_in, rwkv_mu, rwkv_w0, rwkv_w2, rwkv_a0, rwkv_a2, rwkv_g2, rwkv_k_k, rwkv_k_a, rwkv_r_k, rwkv_ln_w, rwkv_ln_b, w_rwkv_out, att_q_g, att_k_g, att_rel_bias, w_att_out, mem_norm_g, w_mem_kv, mem_q_g, mem_k_g, w_mem_out, w_o, norm_ffn_g, w_router_group, b_router_group, w_router_expert, b_router_expert, w_exp_gate, w_exp_up, w_exp_down):
    lp = dict(rwkv_mu=rwkv_mu[0], rwkv_w0=rwkv_w0[0], rwkv_w2=rwkv_w2[0], rwkv_a0=rwkv_a0[0],
              rwkv_a2=rwkv_a2[0], rwkv_g2=rwkv_g2[0], rwkv_k_k=rwkv_k_k[0], rwkv_k_a=rwkv_k_a[0],
              rwkv_r_k=rwkv_r_k[0], rwkv_ln_w=rwkv_ln_w[0], rwkv_ln_b=rwkv_ln_b[0])
    n_samp = DEC_BATCH * DEC_SEQ
    x = jnp.concatenate([x_prompt.reshape(SEQ, D_MODEL), x_sample.reshape(n_samp, D_MODEL)], axis=0)

    h = _rmsnorm(x, norm_mix_g[0], 264, BF16)
    w_in0 = w_in[0]
    c1 = RW_COLS
    c2 = c1 + 3 * ATT_WIDTH
    c3 = c2 + MEM_WIDTH
    w_rw = jnp.pad(w_in0[:, :c1].astype(BF16), ((0, 0), (0, RW_COLS_PAD - RW_COLS)))
    z_rw = _matmul(_mm_plain_kernel, h, w_rw, (), (), tm=1056, tn=512, out_dtype=F32, name="proj_rwkv")

    qk_gain = jnp.concatenate([jnp.tile(att_q_g[0], ATT_HEADS), jnp.tile(att_k_g[0], ATT_HEADS),
                               jnp.ones((ATT_WIDTH,), F32)]).reshape(1, 3 * ATT_WIDTH)
    qkv = _matmul(functools.partial(_mm_headnorm_kernel, head=ATT_HEAD_DIM, n_norm_tiles=8),
                  h, w_in0[:, c1:c2].astype(BF16), (qk_gain,),
                  (pl.BlockSpec((1, 512), lambda j, i: (0, j)),),
                  tm=1056, tn=512, out_dtype=F32, name="proj_qkv")
    q_mem = _matmul(functools.partial(_mm_headnorm_kernel, head=MEM_HEAD_DIM, n_norm_tiles=MEM_HEADS),
                    h, w_in0[:, c2:c3].astype(BF16), (jnp.tile(mem_q_g[0], MEM_HEADS).reshape(1, MEM_WIDTH),),
                    (pl.BlockSpec((1, 512), lambda j, i: (0, j)),),
                    tm=1056, tn=512, out_dtype=BF16, name="proj_mem_q")
    gate = _matmul(_mm_sigmoid_kernel, h, w_in0[:, c3:].astype(BF16), (), (),
                   tm=1056, tn=512, out_dtype=BF16, name="proj_gate")

    shift_p = jnp.zeros((1, 1, RW_COLS), F32)
    prep_p = _rwkv_prep(z_rw, shift_p, lp, row0=0, n_seq=1, seq_len=SEQ, tt=128)
    prep_s = _rwkv_prep(z_rw, state_rwkv_shift[0], lp, row0=SEQ, n_seq=DEC_BATCH, seq_len=DEC_SEQ,
                        tt=DEC_SEQ)
    s0_p = jnp.zeros((1, RW_HEADS, RW_HEAD_DIM, RW_HEAD_DIM), F32)
    rw_parts = []
    states = []
    for (r, w, k, v, kk, b, g), s0, n_seq, seq_len, tt in (
            (prep_p, s0_p, 1, SEQ, 128), (prep_s, state_rwkv[0], DEC_BATCH, DEC_SEQ, DEC_SEQ)):
        y, s_new = _rwkv_scan(r, w, k, v, kk, b, jnp.swapaxes(s0, -1, -2),
                              n_seq=n_seq, seq_len=seq_len, tt=tt)
        rw_parts.append(_rwkv_post(y, r, k, v, g, lp, 256))
        states.append(jnp.swapaxes(s_new, -1, -2))
    a_rw = jnp.concatenate(rw_parts, axis=0)

    bias_p = _band_prompt_bias(att_rel_bias[0])
    bias_s1, bias_s2 = _band_sample_bias(att_rel_bias[0])
    att_p = _band_prompt(qkv, bias_p)
    att_s = _band_sample(qkv, cache_att_k[0].reshape(DEC_BATCH, ATT_PAST, ATT_WIDTH),
                         cache_att_v[0].reshape(DEC_BATCH, ATT_PAST, ATT_WIDTH), bias_s1, bias_s2)
    a_att = jnp.concatenate([att_p, att_s], axis=0)

    mem_h = _rmsnorm(mem_prompt.reshape(N_MEM, D_MODEL), mem_norm_g[0], N_MEM, BF16)
    w_kv = w_mem_kv[0]
    mk_p = _matmul(functools.partial(_mm_headnorm_kernel, head=MEM_HEAD_DIM, n_norm_tiles=MEM_HEADS),
                   mem_h, w_kv[:, :MEM_WIDTH].astype(BF16),
                   (jnp.tile(mem_k_g[0], MEM_HEADS).reshape(1, MEM_WIDTH),),
                   (pl.BlockSpec((1, 512), lambda j, i: (0, j)),),
                   tm=N_MEM, tn=512, out_dtype=F32, name="mem_k")
    mv_p = _matmul(_mm_plain_kernel, mem_h, w_kv[:, MEM_WIDTH:].astype(BF16), (), (),
                   tm=N_MEM, tn=512, out_dtype=F32, name="mem_v")
    mem_p = _mem_attn(q_mem, mk_p.reshape(1, N_MEM, MEM_WIDTH), mv_p.reshape(1, N_MEM, MEM_WIDTH),
                      row0=0, n_rows=SEQ, tq=512, rows_per_mem=SEQ)
    mem_s = _mem_attn(q_mem, cache_mem_k[0].reshape(DEC_BATCH, N_MEM, MEM_WIDTH),
                      cache_mem_v[0].reshape(DEC_BATCH, N_MEM, MEM_WIDTH),
                      row0=SEQ, n_rows=n_samp, tq=DEC_SEQ, rows_per_mem=DEC_SEQ)
    a_mem = jnp.concatenate([mem_p, mem_s], axis=0)

    merged = _gated_merge(a_rw, a_att, a_mem, w_rwkv_out[0].astype(BF16), w_att_out[0].astype(BF16),
                          w_mem_out[0].astype(BF16), gate, 528, 512)
    x1 = _matmul(_mm_residual_kernel, merged, w_o[0].astype(BF16), (x,),
                 (pl.BlockSpec((1056, 512), lambda j, i: (i, j)),),
                 tm=1056, tn=512, out_dtype=F32, name="proj_out")

    w_router = jnp.zeros((D_MODEL, ROUTER_COLS), F32)
    w_router = w_router.at[:, :N_GROUPS].set(w_router_group[0])
    w_router = w_router.at[:, N_GROUPS:N_GROUPS + N_EXPERTS].set(w_router_expert[0])
    b_router = jnp.zeros((1, ROUTER_COLS), F32)
    b_router = b_router.at[0, :N_GROUPS].set(b_router_group[0])
    b_router = b_router.at[0, N_GROUPS:N_GROUPS + N_EXPERTS].set(b_router_expert[0])
    h2, route_id, route_w = _norm_router(x1, norm_ffn_g[0], w_router, b_router, 264)
    in_tok, pos, blk_e, blk_ok, blk_x0 = _moe_layout(route_id[:, :TOP_K])
    yb = _expert_ffn(jnp.take(h2, in_tok, axis=0, mode="clip"), blk_e, blk_ok, blk_x0,
                     w_exp_gate[0], w_exp_up[0], w_exp_down[0])
    take_rows = lambda idx: jnp.take(yb, idx, axis=0, mode="clip")

    def combine(rows):
        return x1[rows] + (route_w[rows, 0:1] * take_rows(pos[rows, 0])
                           + route_w[rows, 1:2] * take_rows(pos[rows, 1]))

    y_prompt = combine(slice(0, SEQ))
    y_sample = combine(slice(SEQ, N_TOK))

    heads = lambda t, bsz, rows: t.reshape(1, bsz, rows, ATT_HEADS, ATT_HEAD_DIM)
    k_cols = slice(ATT_WIDTH, 2 * ATT_WIDTH)
    v_cols = slice(2 * ATT_WIDTH, 3 * ATT_WIDTH)
    keep = SEQ - N_PREV_CHUNKS * CHUNK
    z_s = z_rw[SEQ:, :RW_COLS].reshape(DEC_BATCH, DEC_SEQ, RW_COLS)
    return (
        y_prompt.reshape(1, SEQ, D_MODEL),
        y_sample.reshape(DEC_BATCH, DEC_SEQ, D_MODEL),
        heads(qkv[keep:SEQ, k_cols], 1, SEQ - keep),
        heads(qkv[keep:SEQ, v_cols], 1, SEQ - keep),
        states[0].reshape(1, 1, RW_HEADS, RW_HEAD_DIM, RW_HEAD_DIM),
        z_rw[SEQ - 1:SEQ, :RW_COLS].reshape(1, 1, 1, RW_COLS),
        mk_p.reshape(1, 1, N_MEM, MEM_HEADS, MEM_HEAD_DIM),
        mv_p.reshape(1, 1, N_MEM, MEM_HEADS, MEM_HEAD_DIM),
        heads(qkv[SEQ:, k_cols], DEC_BATCH, DEC_SEQ),
        heads(qkv[SEQ:, v_cols], DEC_BATCH, DEC_SEQ),
        states[1].reshape(1, DEC_BATCH, RW_HEADS, RW_HEAD_DIM, RW_HEAD_DIM),
        z_s[:, -1:, :].reshape(1, DEC_BATCH, 1, RW_COLS),
    )
```

```python
import functools
import math

import numpy as np
import jax
import jax.numpy as jnp
from jax import lax
from jax.experimental import pallas as pl
from jax.experimental.pallas import tpu as pltpu

F32 = jnp.float32
BF16 = jnp.bfloat16
HIGHEST = lax.Precision.HIGHEST

D_MODEL = 4096
SEQ = 8192
DEC_BATCH = 16
DEC_SEQ = 16
PAST_LEN = 2048
N_TOK = SEQ + DEC_BATCH * DEC_SEQ
CHUNK = 64
N_PREV_CHUNKS = 8
RW_HEADS = 32
RW_HEAD_DIM = 64
RW_WIDTH = 2048
DECAY_LORA = 64
ICL_LORA = 64
GATE_LORA = 256
RW_LN_EPS = 64e-5
RW_COLS = 3 * RW_WIDTH + DECAY_LORA + ICL_LORA + GATE_LORA
RW_COLS_PAD = 6656
ATT_HEADS = 16
ATT_HEAD_DIM = 128
ATT_WIDTH = 2048
ATT_PAST = 512
REL_CLIP = 256
N_MEM = 256
MEM_HEADS = 4
MEM_HEAD_DIM = 512
MEM_WIDTH = 2048
N_GROUPS = 8
EXPERTS_PER_GROUP = 8
N_EXPERTS = 64
TOP_K = 2
D_EXPERT = 1024
NORM_EPS = 1e-6
NEG_INF = -1e30

LANES = 128
VMEM_LIMIT = 48 * 1024 * 1024
MOE_VMEM_LIMIT = 54 * 1024 * 1024
MOE_ROWS = 512
MOE_FCHUNK = 256
MOE_DOWN_COLS = 1024


def _params(sem, limit=VMEM_LIMIT):
    return pltpu.CompilerParams(dimension_semantics=sem, vmem_limit_bytes=limit)


def _sigmoid(x):
    return 1.0 / (1.0 + jnp.exp(-x))


def _rmsnorm_kernel(x_ref, g_ref, o_ref):
    x = x_ref[...]
    ms = jnp.mean(x * x, axis=-1, keepdims=True)
    o_ref[...] = (x * lax.rsqrt(ms + NORM_EPS) * g_ref[...]).astype(o_ref.dtype)


def _rmsnorm(x, g, tm, out_dtype):
    m, d = x.shape
    return pl.pallas_call(
        _rmsnorm_kernel,
        grid=(m // tm,),
        in_specs=[pl.BlockSpec((tm, d), lambda i: (i, 0)),
                  pl.BlockSpec((1, d), lambda i: (0, 0))],
        out_specs=pl.BlockSpec((tm, d), lambda i: (i, 0)),
        out_shape=jax.ShapeDtypeStruct((m, d), out_dtype),
        compiler_params=_params(("parallel",)),
        name="rmsnorm",
    )(x, g.reshape(1, d))


def _mm_plain_kernel(x_ref, w_ref, o_ref):
    o_ref[...] = jnp.dot(x_ref[...], w_ref[...], preferred_element_type=F32).astype(o_ref.dtype)


def _mm_headnorm_kernel(x_ref, w_ref, g_ref, o_ref, *, head, n_norm_tiles):
    acc = jnp.dot(x_ref[...], w_ref[...], preferred_element_type=F32)
    j = pl.program_id(0)
    tn = acc.shape[1]

    @pl.when(j < n_norm_tiles)
    def _():
        for c in range(tn // head):
            blk = acc[:, c * head:(c + 1) * head]
            ms = jnp.mean(blk * blk, axis=-1, keepdims=True)
            o_ref[:, c * head:(c + 1) * head] = (
                blk * lax.rsqrt(ms + NORM_EPS) * g_ref[:, c * head:(c + 1) * head]).astype(o_ref.dtype)

    @pl.when(j >= n_norm_tiles)
    def _():
        o_ref[...] = acc.astype(o_ref.dtype)


def _mm_sigmoid_kernel(x_ref, w_ref, o_ref):
    acc = jnp.dot(x_ref[...], w_ref[...], preferred_element_type=F32)
    o_ref[...] = _sigmoid(acc).astype(o_ref.dtype)


def _mm_residual_kernel(x_ref, w_ref, r_ref, o_ref):
    o_ref[...] = r_ref[...] + jnp.dot(x_ref[...], w_ref[...], preferred_element_type=F32)


def _matmul(body, x, w, extras, extra_specs, *, tm, tn, out_dtype, name):
    m, k = x.shape
    n = w.shape[1]
    return pl.pallas_call(
        body,
        grid=(n // tn, m // tm),
        in_specs=[pl.BlockSpec((tm, k), lambda j, i: (i, 0)),
                  pl.BlockSpec((k, tn), lambda j, i: (0, j))] + list(extra_specs),
        out_specs=pl.BlockSpec((tm, tn), lambda j, i: (i, j)),
        out_shape=jax.ShapeDtypeStruct((m, n), out_dtype),
        compiler_params=_params(("parallel", "parallel")),
        name=name,
    )(x, w, *extras)


def _merge_kernel(a0_ref, a1_ref, a2_ref, w0_ref, w1_ref, w2_ref, g0_ref, g1_ref, g2_ref, o_ref):
    y0 = jnp.dot(a0_ref[...], w0_ref[...], preferred_element_type=F32)
    y1 = jnp.dot(a1_ref[...], w1_ref[...], preferred_element_type=F32)
    y2 = jnp.dot(a2_ref[...], w2_ref[...], preferred_element_type=F32)
    merged = (g0_ref[...].astype(F32) * y0 + g1_ref[...].astype(F32) * y1
              + g2_ref[...].astype(F32) * y2)
    o_ref[...] = merged.astype(o_ref.dtype)


def _gated_merge(a_rw, a_att, a_mem, w_rw, w_att, w_mem, gate, tm, tn):
    m, k = a_rw.shape
    n = w_rw.shape[1]
    nj = n // tn
    a_spec = pl.BlockSpec((tm, k), lambda j, i: (i, 0))
    w_spec = pl.BlockSpec((k, tn), lambda j, i: (0, j))
    g_specs = [pl.BlockSpec((tm, tn), functools.partial(lambda j, i, b: (i, b * nj + j), b=b))
               for b in range(3)]
    return pl.pallas_call(
        _merge_kernel,
        grid=(nj, m // tm),
        in_specs=[a_spec, a_spec, a_spec, w_spec, w_spec, w_spec] + g_specs,
        out_specs=pl.BlockSpec((tm, tn), lambda j, i: (i, j)),
        out_shape=jax.ShapeDtypeStruct((m, n), BF16),
        compiler_params=_params(("parallel", "parallel")),
        name="gated_merge",
    )(a_rw, a_att, a_mem, w_rw, w_att, w_mem, gate, gate, gate)


def _half_ones():
    qi = lax.broadcasted_iota(jnp.int32, (LANES, LANES), 0) // RW_HEAD_DIM
    qj = lax.broadcasted_iota(jnp.int32, (LANES, LANES), 1) // RW_HEAD_DIM
    return qi == qj


def _rwkv_prep_kernel(z_ref, zp_ref, sh_ref, mu_ref, w0_ref, a0_ref, kk_ref, ka_ref,
                      w2_ref, a2_ref, g2_ref,
                      r_o, w_o, k_o, v_o, kk_o, b_o, g_o, *, tiles_per_seq):
    i = pl.program_id(0)
    first = (i % tiles_per_seq) == 0
    z = z_ref[...]
    prev_row = jnp.where(first, sh_ref[...], zp_ref[7:8, :])
    rows = lax.broadcasted_iota(jnp.int32, z.shape, 0)
    z_prev = jnp.where(rows == 0, prev_row, pltpu.roll(z, 1, 0))
    zs = z + (z_prev - z) * mu_ref[...]

    c1, c2, c3 = RW_WIDTH, 2 * RW_WIDTH, 3 * RW_WIDTH
    c4, c5 = c3 + DECAY_LORA, c3 + DECAY_LORA + ICL_LORA
    r = zs[:, :c1]
    k = zs[:, c1:c2]
    v = zs[:, c2:c3]
    wl = w0_ref[...] + jnp.dot(jnp.tanh(zs[:, c3:c4]), w2_ref[...], precision=HIGHEST,
                               preferred_element_type=F32)
    decay = jnp.exp(-math.exp(-0.5) * _sigmoid(wl))
    a = _sigmoid(a0_ref[...] + jnp.dot(zs[:, c4:c5], a2_ref[...], precision=HIGHEST,
                                       preferred_element_type=F32))
    g = jnp.dot(_sigmoid(zs[:, c5:]), g2_ref[...], precision=HIGHEST, preferred_element_type=F32)

    q = _half_ones().astype(F32)
    kk = k * kk_ref[...]
    for c in range(RW_WIDTH // LANES):
        sl = slice(c * LANES, (c + 1) * LANES)
        blk = kk[:, sl]
        ss = jnp.dot(blk * blk, q, precision=HIGHEST, preferred_element_type=F32)
        blk = blk / jnp.maximum(jnp.sqrt(ss), 1e-12)
        kk_o[:, sl] = blk
        b_o[:, sl] = blk * a[:, sl]
    r_o[...] = r
    w_o[...] = decay
    k_o[...] = k * (1.0 + (a - 1.0) * ka_ref[...])
    v_o[...] = v
    g_o[...] = g


def _rwkv_prep(z_rw, shift, lp, *, row0, n_seq, seq_len, tt):
    tiles_per_seq = seq_len // tt
    n_tiles = n_seq * tiles_per_seq
    off = row0 // tt
    off8 = row0 // 8
    row = lambda a: a.reshape(1, -1)
    full = lambda shape: pl.BlockSpec(shape, lambda i: (0, 0))
    out_sd = jax.ShapeDtypeStruct((n_seq * seq_len, RW_WIDTH), F32)
    o_spec = pl.BlockSpec((tt, RW_WIDTH), lambda i: (i, 0))
    return pl.pallas_call(
        functools.partial(_rwkv_prep_kernel, tiles_per_seq=tiles_per_seq),
        grid=(n_tiles,),
        in_specs=[
            pl.BlockSpec((tt, RW_COLS), lambda i: (off + i, 0)),
            pl.BlockSpec((8, RW_COLS), lambda i: (jnp.maximum(off8 + i * (tt // 8) - 1, 0), 0)),
            pl.BlockSpec((None, 1, RW_COLS), lambda i: (i // tiles_per_seq, 0, 0)),
            full((1, RW_COLS)), full((1, RW_WIDTH)), full((1, RW_WIDTH)), full((1, RW_WIDTH)),
            full((1, RW_WIDTH)), full((DECAY_LORA, RW_WIDTH)), full((ICL_LORA, RW_WIDTH)),
            full((GATE_LORA, RW_WIDTH)),
        ],
        out_specs=[o_spec] * 7,
        out_shape=[out_sd] * 7,
        compiler_params=_params(("parallel",)),
        name="rwkv_prep",
    )(z_rw, z_rw, shift, row(lp['rwkv_mu']), row(lp['rwkv_w0']), row(lp['rwkv_a0']),
      row(lp['rwkv_k_k']), row(lp['rwkv_k_a']), lp['rwkv_w2'], lp['rwkv_a2'], lp['rwkv_g2'])


RW_CHUNK = 8


def _split_bf16(x):
    hi = x.astype(BF16)
    return hi, (x - hi.astype(F32)).astype(BF16)


def _rwkv_scan_kernel(r_ref, w_ref, k_ref, v_ref, kk_ref, b_ref, s0_ref, y_ref, st_ref, s_scr,
                      *, tt, pairs):
    c = RW_CHUNK
    tb = pl.program_id(1)
    n_tb = pl.num_programs(1)
    left64 = lax.broadcasted_iota(jnp.int32, (RW_HEAD_DIM, LANES), 1) < RW_HEAD_DIM
    row_c = lax.broadcasted_iota(jnp.int32, (c, LANES), 0)
    q = _half_ones().astype(BF16)
    q2 = jnp.concatenate([q, q], axis=0)
    row_2c = lax.broadcasted_iota(jnp.int32, (2 * c, LANES), 0)
    ones_rows = jnp.where(row_2c < 2, 1.0, 0.0).astype(BF16)

    def segsum(x):
        hi, lo = _split_bf16(x)
        return jnp.dot(jnp.concatenate([hi, lo], axis=1), q2, preferred_element_type=F32)

    def shift_rows(x, n, fill):
        return jnp.where(row_c >= n, pltpu.roll(x, n, 0), fill)

    def per_head_blocks(x):
        zero = jnp.zeros_like(x)
        return jnp.concatenate([jnp.where(left64, x, zero), jnp.where(left64, zero, x)], axis=0)

    def own_head(x):
        return jnp.where(left64, x[:RW_HEAD_DIM], x[RW_HEAD_DIM:])

    @pl.when(tb == 0)
    def _():
        for p in range(pairs):
            s_scr[p] = jnp.concatenate([s0_ref[0, 2 * p], s0_ref[0, 2 * p + 1]], axis=1)

    zero_2c = jnp.zeros((2 * c, LANES), BF16)

    def chunk(ci, carry):
        rows = pl.ds(pl.multiple_of(ci * c, c), c)
        lanes = [pl.ds(p * LANES, LANES) for p in range(pairs)]
        vs, keys, states, grams, from_state = [], [], [], [], []
        for p in range(pairs):
            r, w, k, v, kk, b = (ref[rows, lanes[p]] for ref in (r_ref, w_ref, k_ref, v_ref, kk_ref, b_ref))
            gam = w
            n = 1
            while n < c:
                gam = gam * shift_rows(gam, n, 1.0)
                n *= 2
            inv = 1.0 / gam
            a_t = -(shift_rows(gam, 1, 1.0) * kk)
            b_t = b * inv
            k_t = k * inv
            r_t = gam * r
            g_end = gam[c - 1:c, :]
            prods = []
            for s in range(c):
                bs, ks = b_t[s:s + 1, :], k_t[s:s + 1, :]
                prods += [a_t * bs, a_t * ks, r_t * bs, r_t * ks]
            grams.append(segsum(jnp.concatenate(prods, axis=0)))
            vs.append(v)
            keys.append((jnp.concatenate([b_t, k_t], axis=0) * g_end, g_end,
                         _split_bf16(jnp.concatenate([a_t, r_t], axis=0))))
        for p in range(pairs):
            st = s_scr[p]
            s_hi, s_lo = _split_bf16(st)
            ar_hi, ar_lo = keys[p][2]
            states.append(st)
            from_state.append(jnp.dot(
                jnp.concatenate([ar_hi, ar_hi, ar_lo], axis=1),
                jnp.concatenate([per_head_blocks(s_hi), per_head_blocks(s_lo), per_head_blocks(s_hi)], axis=0),
                preferred_element_type=F32))
        operands = []
        for p in range(pairs):
            v = vs[p]
            g = lambda s, m: grams[p][(4 * s + m) * c:(4 * s + m + 1) * c, :]
            d = from_state[p][:c]
            y = from_state[p][c:]
            for s in range(c - 1):
                d = d + jnp.where(row_c > s, g(s, 1), 0.0) * v[s:s + 1, :]
            sa_rows = []
            for s in range(c):
                sa = d[s:s + 1, :]
                sa_rows.append(sa)
                if s < c - 1:
                    d = d + g(s, 0) * sa
                y = (y + jnp.where(row_c >= s, g(s, 2), 0.0) * sa
                     + jnp.where(row_c >= s, g(s, 3), 0.0) * v[s:s + 1, :])
            y_ref[rows, lanes[p]] = y
            key, g_end, _ = keys[p]
            key_hi, key_lo = _split_bf16(key)
            val_hi, val_lo = _split_bf16(jnp.concatenate(sa_rows + [v], axis=0))
            g_hi = g_end.astype(BF16).astype(F32)
            g_rows = jnp.where(row_2c == 0, g_hi, jnp.where(row_2c == 1, g_end - g_hi, 0.0)).astype(BF16)
            operands.append((
                jnp.concatenate([key_hi, key_hi, key_lo, g_rows], axis=0),
                jnp.concatenate([
                    jnp.concatenate([val_hi, val_lo, val_hi, zero_2c], axis=0),
                    jnp.concatenate([zero_2c, zero_2c, zero_2c, ones_rows], axis=0)], axis=1)))
        for p in range(pairs):
            out = lax.dot_general(operands[p][0], operands[p][1], (((0,), (0,)), ((), ())),
                                  preferred_element_type=F32)
            s_scr[p] = own_head(out[:, LANES:]) * states[p] + own_head(out[:, :LANES])
        return carry

    lax.fori_loop(0, tt // c, chunk, 0)

    @pl.when(tb == n_tb - 1)
    def _():
        for p in range(pairs):
            st_ref[0, 2 * p] = s_scr[p][:, :RW_HEAD_DIM]
            st_ref[0, 2 * p + 1] = s_scr[p][:, RW_HEAD_DIM:]


def _rwkv_scan(r, w, k, v, kk, b, s0, *, n_seq, seq_len, tt):
    n_tb = seq_len // tt
    pairs = RW_HEADS // 2
    t_spec = pl.BlockSpec((tt, RW_WIDTH), lambda s, t: (s * n_tb + t, 0))
    s_spec = pl.BlockSpec((1, RW_HEADS, RW_HEAD_DIM, RW_HEAD_DIM), lambda s, t: (s, 0, 0, 0))
    return pl.pallas_call(
        functools.partial(_rwkv_scan_kernel, tt=tt, pairs=pairs),
        grid=(n_seq, n_tb),
        in_specs=[t_spec] * 6 + [s_spec],
        out_specs=[t_spec, s_spec],
        out_shape=[jax.ShapeDtypeStruct((n_seq * seq_len, RW_WIDTH), F32),
                   jax.ShapeDtypeStruct((n_seq, RW_HEADS, RW_HEAD_DIM, RW_HEAD_DIM), F32)],
        scratch_shapes=[pltpu.VMEM((pairs, RW_HEAD_DIM, LANES), F32)],
        compiler_params=_params(("parallel", "arbitrary")),
        name="rwkv_scan",
    )(r, w, k, v, kk, b, s0)


def _rwkv_post_kernel(y_ref, r_ref, k_ref, v_ref, g_ref, lnw_ref, lnb_ref, rk_ref, o_ref):
    q = _half_ones().astype(F32)
    seg = lambda x: jnp.dot(x, q, precision=HIGHEST, preferred_element_type=F32)
    for c in range(RW_WIDTH // LANES):
        sl = slice(c * LANES, (c + 1) * LANES)
        y = y_ref[:, sl]
        mean = seg(y) * (1.0 / RW_HEAD_DIM)
        d = y - mean
        var = seg(d * d) * (1.0 / RW_HEAD_DIM)
        yn = d * lax.rsqrt(var + RW_LN_EPS) * lnw_ref[:, sl] + lnb_ref[:, sl]
        bonus = seg(r_ref[:, sl] * k_ref[:, sl] * rk_ref[:, sl]) * v_ref[:, sl]
        o_ref[:, sl] = ((yn + bonus) * g_ref[:, sl]).astype(o_ref.dtype)


def _rwkv_post(y, r, k, v, g, lp, tt):
    m = y.shape[0]
    row = lambda a: a.reshape(1, -1)
    t_spec = pl.BlockSpec((tt, RW_WIDTH), lambda i: (i, 0))
    p_spec = pl.BlockSpec((1, RW_WIDTH), lambda i: (0, 0))
    return pl.pallas_call(
        _rwkv_post_kernel,
        grid=(m // tt,),
        in_specs=[t_spec] * 5 + [p_spec] * 3,
        out_specs=t_spec,
        out_shape=jax.ShapeDtypeStruct((m, RW_WIDTH), BF16),
        compiler_params=_params(("parallel",)),
        name="rwkv_post",
    )(y, r, k, v, g, row(lp['rwkv_ln_w']), row(lp['rwkv_ln_b']), row(lp['rwkv_r_k']))


BAND_TQ = 512


def _softmax2(s1, s2):
    m = jnp.maximum(jnp.max(s1, axis=-1, keepdims=True), jnp.max(s2, axis=-1, keepdims=True))
    p1 = jnp.exp(s1 - m)
    p2 = jnp.exp(s2 - m)
    denom = jnp.sum(p1, axis=-1, keepdims=True) + jnp.sum(p2, axis=-1, keepdims=True)
    return p1, p2, denom


def _dot_nt(a, b):
    return lax.dot_general(a, b, (((1,), (1,)), ((), ())), preferred_element_type=F32)


def _band_prompt_kernel(q_ref, kp_ref, kc_ref, vp_ref, vc_ref, bias_ref, o_ref):
    i = pl.program_id(1)
    scale = ATT_HEAD_DIM ** -0.5
    q = q_ref[...].astype(BF16)
    s1 = _dot_nt(q, kp_ref[...].astype(BF16)) * scale + bias_ref[:, :BAND_TQ]
    s2 = _dot_nt(q, kc_ref[...].astype(BF16)) * scale + bias_ref[:, BAND_TQ:]
    s1 = jnp.where(i == 0, NEG_INF, s1)
    p1, p2, denom = _softmax2(s1, s2)
    o = (jnp.dot(p1.astype(BF16), vp_ref[...].astype(BF16), preferred_element_type=F32)
         + jnp.dot(p2.astype(BF16), vc_ref[...].astype(BF16), preferred_element_type=F32))
    o_ref[...] = (o / denom).astype(o_ref.dtype)


def _band_prompt(qkv, bias):
    nq = SEQ // BAND_TQ
    prev = lambda h, i: jnp.maximum(i - 1, 0)
    blk = (BAND_TQ, ATT_HEAD_DIM)
    return pl.pallas_call(
        _band_prompt_kernel,
        grid=(ATT_HEADS, nq),
        in_specs=[
            pl.BlockSpec(blk, lambda h, i: (i, h)),
            pl.BlockSpec(blk, lambda h, i: (prev(h, i), ATT_HEADS + h)),
            pl.BlockSpec(blk, lambda h, i: (i, ATT_HEADS + h)),
            pl.BlockSpec(blk, lambda h, i: (prev(h, i), 2 * ATT_HEADS + h)),
            pl.BlockSpec(blk, lambda h, i: (i, 2 * ATT_HEADS + h)),
            pl.BlockSpec((None, BAND_TQ, 2 * BAND_TQ), lambda h, i: (h, 0, 0)),
        ],
        out_specs=pl.BlockSpec(blk, lambda h, i: (i, h)),
        out_shape=jax.ShapeDtypeStruct((SEQ, ATT_WIDTH), BF16),
        compiler_params=_params(("parallel", "parallel")),
        name="band_attention_prompt",
    )(qkv, qkv, qkv, qkv, qkv, bias)


def _band_prompt_bias(rel_bias):
    iq = np.arange(BAND_TQ)[:, None]
    ik = np.arange(2 * BAND_TQ)[None, :]
    period = 3 * BAND_TQ
    m = np.arange(period)
    k_minus_q = np.where(m <= 2 * BAND_TQ, m, m - period)
    idx = np.clip(BAND_TQ - k_minus_q, -REL_CLIP, REL_CLIP) + REL_CLIP
    vec = jnp.take(rel_bias, jnp.asarray(idx), axis=1)
    flat = jnp.broadcast_to(vec[:, None, :], (ATT_HEADS, BAND_TQ, period)).reshape(ATT_HEADS, -1)
    table = flat[:, :BAND_TQ * (period - 1)].reshape(ATT_HEADS, BAND_TQ, period - 1)[:, :, :2 * BAND_TQ]
    qc = iq // CHUNK + N_PREV_CHUNKS
    kc = ik // CHUNK
    mask = (kc <= qc) & (kc >= qc - N_PREV_CHUNKS)
    return jnp.where(jnp.asarray(mask)[None], table, NEG_INF).astype(F32)


def _band_sample_kernel(q_ref, k_ref, v_ref, kc_ref, vc_ref, b1_ref, b2_ref, o_ref):
    scale = ATT_HEAD_DIM ** -0.5
    for h in range(ATT_HEADS):
        sl = slice(h * ATT_HEAD_DIM, (h + 1) * ATT_HEAD_DIM)
        q = q_ref[:, sl].astype(BF16)
        s1 = _dot_nt(q, kc_ref[:, sl].astype(BF16)) * scale + b1_ref[h]
        s2 = _dot_nt(q, k_ref[:, sl].astype(BF16)) * scale + b2_ref[h]
        p1, p2, denom = _softmax2(s1, s2)
        o = (jnp.dot(p1.astype(BF16), vc_ref[:, sl].astype(BF16), preferred_element_type=F32)
             + jnp.dot(p2.astype(BF16), v_ref[:, sl].astype(BF16), preferred_element_type=F32))
        o_ref[:, sl] = (o / denom).astype(o_ref.dtype)


def _band_sample(qkv, cache_k, cache_v, bias_past, bias_new):
    row0 = SEQ // DEC_SEQ
    new = lambda c: pl.BlockSpec((DEC_SEQ, ATT_WIDTH), lambda b: (row0 + b, c))
    cache = pl.BlockSpec((None, ATT_PAST, ATT_WIDTH), lambda b: (b, 0, 0))
    return pl.pallas_call(
        _band_sample_kernel,
        grid=(DEC_BATCH,),
        in_specs=[new(0), new(1), new(2), cache, cache,
                  pl.BlockSpec((ATT_HEADS, DEC_SEQ, ATT_PAST), lambda b: (0, 0, 0)),
                  pl.BlockSpec((ATT_HEADS, DEC_SEQ, DEC_SEQ), lambda b: (0, 0, 0))],
        out_specs=pl.BlockSpec((DEC_SEQ, ATT_WIDTH), lambda b: (b, 0)),
        out_shape=jax.ShapeDtypeStruct((DEC_BATCH * DEC_SEQ, ATT_WIDTH), BF16),
        compiler_params=_params(("parallel",)),
        name="band_attention_sample",
    )(qkv, qkv, qkv, cache_k, cache_v, bias_past, bias_new)


def _band_sample_bias(rel_bias):
    q_pos = PAST_LEN + np.arange(DEC_SEQ)[:, None]
    k_pos = PAST_LEN - ATT_PAST + np.arange(ATT_PAST + DEC_SEQ)[None, :]
    rel = np.clip(q_pos - k_pos, -REL_CLIP, REL_CLIP) + REL_CLIP
    qc, kc = q_pos // CHUNK, k_pos // CHUNK
    mask = (k_pos >= 0) & (kc <= qc) & (kc >= qc - N_PREV_CHUNKS)
    table = jnp.take(rel_bias, jnp.asarray(rel.reshape(-1)), axis=1).reshape(
        ATT_HEADS, DEC_SEQ, ATT_PAST + DEC_SEQ)
    table = jnp.where(jnp.asarray(mask)[None], table, NEG_INF).astype(F32)
    return table[:, :, :ATT_PAST], table[:, :, ATT_PAST:]


def _mem_attn_kernel(q_ref, k_ref, v_ref, o_ref):
    s = _dot_nt(q_ref[...], k_ref[...].astype(BF16)) * (MEM_HEAD_DIM ** -0.5)
    m = jnp.max(s, axis=-1, keepdims=True)
    p = jnp.exp(s - m)
    denom = jnp.sum(p, axis=-1, keepdims=True)
    o = jnp.dot(p.astype(BF16), v_ref[...].astype(BF16), preferred_element_type=F32)
    o_ref[...] = (o / denom).astype(o_ref.dtype)


def _mem_attn(q_mem, mem_k, mem_v, *, row0, n_rows, tq, rows_per_mem):
    off = row0 // tq
    kv_spec = pl.BlockSpec((None, N_MEM, MEM_HEAD_DIM), lambda h, i: ((i * tq) // rows_per_mem, 0, h))
    return pl.pallas_call(
        _mem_attn_kernel,
        grid=(MEM_HEADS, n_rows // tq),
        in_specs=[pl.BlockSpec((tq, MEM_HEAD_DIM), lambda h, i: (off + i, h)), kv_spec, kv_spec],
        out_specs=pl.BlockSpec((tq, MEM_HEAD_DIM), lambda h, i: (i, h)),
        out_shape=jax.ShapeDtypeStruct((n_rows, MEM_WIDTH), BF16),
        compiler_params=_params(("parallel", "parallel")),
        name="memory_attention",
    )(q_mem, mem_k, mem_v)


ROUTER_COLS = 128


def _norm_router_kernel(x_ref, g_ref, wr_ref, br_ref, h_ref, id_ref, wt_ref):
    x = x_ref[...]
    ms = jnp.mean(x * x, axis=-1, keepdims=True)
    h = x * lax.rsqrt(ms + NORM_EPS) * g_ref[...]
    h_ref[...] = h.astype(h_ref.dtype)
    logits = jnp.dot(h, wr_ref[...], precision=HIGHEST, preferred_element_type=F32) + br_ref[...]

    lane = lax.broadcasted_iota(jnp.int32, logits.shape, 1).astype(F32)
    row_max = lambda a: jnp.max(a, axis=-1, keepdims=True)
    first_at = lambda a, top: jnp.min(jnp.where(a == top, lane, float(ROUTER_COLS)), axis=-1, keepdims=True)
    is_group = lane < N_GROUPS
    g_logit = jnp.where(is_group, logits, NEG_INF)
    g_top = row_max(g_logit)
    g_idx = first_at(g_logit, g_top)
    p_group = 1.0 / jnp.sum(jnp.where(is_group, jnp.exp(logits - g_top), 0.0), axis=-1, keepdims=True)
    lo = N_GROUPS + g_idx * EXPERTS_PER_GROUP
    e_logit = jnp.where((lane >= lo) & (lane < lo + EXPERTS_PER_GROUP), logits, NEG_INF)
    e1 = row_max(e_logit)
    i1 = first_at(e_logit, e1)
    e_rest = jnp.where(lane == i1, NEG_INF, e_logit)
    e2 = row_max(e_rest)
    i2 = first_at(e_rest, e2)
    t = jnp.exp(e2 - e1)
    w1 = p_group / (1.0 + t)
    w2 = p_group * t / (1.0 + t)
    id_ref[...] = jnp.where(lane == 0, i1 - N_GROUPS, jnp.where(lane == 1, i2 - N_GROUPS, 0.0)).astype(jnp.int32)
    wt_ref[...] = jnp.where(lane == 0, w1, jnp.where(lane == 1, w2, 0.0))


def _norm_router(x, g, w_router, b_router, tm):
    m, d = x.shape
    return pl.pallas_call(
        _norm_router_kernel,
        grid=(m // tm,),
        in_specs=[pl.BlockSpec((tm, d), lambda i: (i, 0)),
                  pl.BlockSpec((1, d), lambda i: (0, 0)),
                  pl.BlockSpec((d, ROUTER_COLS), lambda i: (0, 0)),
                  pl.BlockSpec((1, ROUTER_COLS), lambda i: (0, 0))],
        out_specs=[pl.BlockSpec((tm, d), lambda i: (i, 0)),
                   pl.BlockSpec((tm, ROUTER_COLS), lambda i: (i, 0)),
                   pl.BlockSpec((tm, ROUTER_COLS), lambda i: (i, 0))],
        out_shape=[jax.ShapeDtypeStruct((m, d), BF16),
                   jax.ShapeDtypeStruct((m, ROUTER_COLS), jnp.int32),
                   jax.ShapeDtypeStruct((m, ROUTER_COLS), F32)],
        compiler_params=_params(("parallel",)),
        name="ffn_norm_router",
    )(x, g.reshape(1, d), w_router, b_router)


MOE_SEG_ALIGN = 16


def _expert_kernel(blk_e_ref, blk_ok_ref, blk_x0_ref, blk_rows_ref, xs_ref, wg_ref, wu_ref, wd_ref, o_ref,
                   xbuf, sem):
    del blk_e_ref
    b = pl.program_id(0)
    f = pl.program_id(1)
    nb = pl.num_programs(0)
    ok = blk_ok_ref[b] > 0
    slot = b % 2

    def x_copy(blk, dst_slot):
        start = pl.multiple_of(blk_x0_ref[blk], MOE_SEG_ALIGN)
        return pltpu.make_async_copy(xs_ref.at[pl.ds(start, MOE_ROWS)], xbuf.at[dst_slot], sem.at[dst_slot])

    @pl.when(f == 0)
    def _():
        o_ref[...] = jnp.zeros_like(o_ref)

    @pl.when(ok & (f == 0))
    def _():
        @pl.when(b == 0)
        def _():
            x_copy(0, 0).start()

        x_copy(b, slot).wait()
        nxt = jnp.minimum(b + 1, nb - 1)

        @pl.when((b + 1 < nb) & (blk_ok_ref[nxt] > 0))
        def _():
            x_copy(nxt, 1 - slot).start()

    def ffn(rows):
        x = xbuf[slot, :rows, :]
        gate = jnp.dot(x, wg_ref[...].astype(BF16), preferred_element_type=F32)
        up = jnp.dot(x, wu_ref[...].astype(BF16), preferred_element_type=F32)
        hidden = (gate * _sigmoid(gate) * up).astype(BF16)
        for n in range(D_MODEL // MOE_DOWN_COLS):
            cols = slice(n * MOE_DOWN_COLS, (n + 1) * MOE_DOWN_COLS)
            o_ref[:rows, cols] += jnp.dot(hidden, wd_ref[:, cols].astype(BF16), preferred_element_type=F32)

    half_full = blk_rows_ref[b] <= MOE_ROWS // 2

    @pl.when(ok & half_full)
    def _():
        ffn(MOE_ROWS // 2)

    @pl.when(ok & jnp.logical_not(half_full))
    def _():
        ffn(MOE_ROWS)


def _expert_ffn(xs, blk_e, blk_ok, blk_x0, blk_rows, w_gate, w_up, w_down):
    nb = blk_e.shape[0]
    nf = D_EXPERT // MOE_FCHUNK
    fidx = lambda b, f, ok: jnp.where(ok[b] > 0, f, nf - 1)
    grid_spec = pltpu.PrefetchScalarGridSpec(
        num_scalar_prefetch=4,
        grid=(nb, nf),
        in_specs=[
            pl.BlockSpec(memory_space=pl.ANY),
            pl.BlockSpec((None, D_MODEL, MOE_FCHUNK), lambda b, f, e, ok, x0, nr: (e[b], 0, fidx(b, f, ok))),
            pl.BlockSpec((None, D_MODEL, MOE_FCHUNK), lambda b, f, e, ok, x0, nr: (e[b], 0, fidx(b, f, ok))),
            pl.BlockSpec((None, MOE_FCHUNK, D_MODEL), lambda b, f, e, ok, x0, nr: (e[b], fidx(b, f, ok), 0)),
        ],
        out_specs=pl.BlockSpec((MOE_ROWS, D_MODEL), lambda b, f, e, ok, x0, nr: (b, 0),
                               pipeline_mode=pl.Buffered(1)),
        scratch_shapes=[pltpu.VMEM((2, MOE_ROWS, D_MODEL), BF16), pltpu.SemaphoreType.DMA((2,))],
    )
    return pl.pallas_call(
        _expert_kernel,
        grid_spec=grid_spec,
        out_shape=jax.ShapeDtypeStruct((nb * MOE_ROWS, D_MODEL), F32),
        compiler_params=_params(("arbitrary", "arbitrary"), MOE_VMEM_LIMIT),
        name="expert_ffn",
    )(blk_e, blk_ok, blk_x0, blk_rows, xs, w_gate, w_up, w_down)


def _moe_layout(expert_id):
    n = expert_id.shape[0]
    a = n * TOP_K
    n_blocks = a // MOE_ROWS + N_EXPERTS
    in_rows = a + N_EXPERTS * MOE_SEG_ALIGN + MOE_ROWS
    flat_e = expert_id.reshape(a)
    flat_tok = jnp.arange(a, dtype=jnp.int32) // TOP_K
    order = jnp.argsort(flat_e)
    sorted_e = flat_e[order]
    counts = jnp.bincount(flat_e, length=N_EXPERTS).astype(jnp.int32)
    start = jnp.cumsum(counts) - counts
    rank = jnp.arange(a, dtype=jnp.int32) - start[sorted_e]
    seg = (counts + MOE_SEG_ALIGN - 1) // MOE_SEG_ALIGN * MOE_SEG_ALIGN
    seg_start = jnp.cumsum(seg) - seg
    padded = (counts + MOE_ROWS - 1) // MOE_ROWS * MOE_ROWS
    pad_end = jnp.cumsum(padded)
    pad_start = pad_end - padded
    in_tok = jnp.zeros((in_rows,), jnp.int32).at[seg_start[sorted_e] + rank].set(flat_tok[order])
    pos = jnp.zeros((a,), jnp.int32).at[order].set(pad_start[sorted_e] + rank).reshape(n, TOP_K)
    blk_start = jnp.arange(n_blocks, dtype=jnp.int32) * MOE_ROWS
    blk_ok = (blk_start < pad_end[-1]).astype(jnp.int32)
    n_ok = jnp.sum(blk_ok)
    blk_e = jnp.minimum(jnp.searchsorted(pad_end, blk_start, side='right'), N_EXPERTS - 1).astype(jnp.int32)
    blk_e = jnp.where(blk_ok > 0, blk_e, blk_e[n_ok - 1])
    blk_x0 = jnp.where(blk_ok > 0, seg_start[blk_e] + (blk_start - pad_start[blk_e]), 0).astype(jnp.int32)
    blk_rows = (jnp.clip(counts[blk_e] - (blk_start - pad_start[blk_e]), 0, MOE_ROWS) * blk_ok).astype(jnp.int32)
    return in_tok, pos, blk_e, blk_ok, blk_x0, blk_rows


def kernel(x_prompt, x_sample, mem_prompt, cache_att_k, cache_att_v, cache_mem_k, cache_mem_v, state_rwkv, state_rwkv_shift, norm_mix_g, w_in, rwkv_mu, rwkv_w0, rwkv_w2, rwkv_a0, rwkv_a2, rwkv_g2, rwkv_k_k, rwkv_k_a, rwkv_r_k, rwkv_ln_w, rwkv_ln_b, w_rwkv_out, att_q_g, att_k_g, att_rel_bias, w_att_out, mem_norm_g, w_mem_kv, mem_q_g, mem_k_g, w_mem_out, w_o, norm_ffn_g, w_router_group, b_router_group, w_router_expert, b_router_expert, w_exp_gate, w_exp_up, w_exp_down):
    lp = dict(rwkv_mu=rwkv_mu[0], rwkv_w0=rwkv_w0[0], rwkv_w2=rwkv_w2[0], rwkv_a0=rwkv_a0[0],
              rwkv_a2=rwkv_a2[0], rwkv_g2=rwkv_g2[0], rwkv_k_k=rwkv_k_k[0], rwkv_k_a=rwkv_k_a[0],
              rwkv_r_k=rwkv_r_k[0], rwkv_ln_w=rwkv_ln_w[0], rwkv_ln_b=rwkv_ln_b[0])
    n_samp = DEC_BATCH * DEC_SEQ
    x = jnp.concatenate([x_prompt.reshape(SEQ, D_MODEL), x_sample.reshape(n_samp, D_MODEL)], axis=0)

    h = _rmsnorm(x, norm_mix_g[0], 264, BF16)
    w_in0 = w_in[0]
    c1 = RW_COLS
    c2 = c1 + 3 * ATT_WIDTH
    c3 = c2 + MEM_WIDTH
    w_rw = jnp.pad(w_in0[:, :c1].astype(BF16), ((0, 0), (0, RW_COLS_PAD - RW_COLS)))
    z_rw = _matmul(_mm_plain_kernel, h, w_rw, (), (), tm=1056, tn=512, out_dtype=F32, name="proj_rwkv")

    qk_gain = jnp.concatenate([jnp.tile(att_q_g[0], ATT_HEADS), jnp.tile(att_k_g[0], ATT_HEADS),
                               jnp.ones((ATT_WIDTH,), F32)]).reshape(1, 3 * ATT_WIDTH)
    qkv = _matmul(functools.partial(_mm_headnorm_kernel, head=ATT_HEAD_DIM, n_norm_tiles=8),
                  h, w_in0[:, c1:c2].astype(BF16), (qk_gain,),
                  (pl.BlockSpec((1, 512), lambda j, i: (0, j)),),
                  tm=1056, tn=512, out_dtype=F32, name="proj_qkv")
    q_mem = _matmul(functools.partial(_mm_headnorm_kernel, head=MEM_HEAD_DIM, n_norm_tiles=MEM_HEADS),
                    h, w_in0[:, c2:c3].astype(BF16), (jnp.tile(mem_q_g[0], MEM_HEADS).reshape(1, MEM_WIDTH),),
                    (pl.BlockSpec((1, 512), lambda j, i: (0, j)),),
                    tm=1056, tn=512, out_dtype=BF16, name="proj_mem_q")
    gate = _matmul(_mm_sigmoid_kernel, h, w_in0[:, c3:].astype(BF16), (), (),
                   tm=1056, tn=512, out_dtype=BF16, name="proj_gate")

    shift_p = jnp.zeros((1, 1, RW_COLS), F32)
    prep_p = _rwkv_prep(z_rw, shift_p, lp, row0=0, n_seq=1, seq_len=SEQ, tt=128)
    prep_s = _rwkv_prep(z_rw, state_rwkv_shift[0], lp, row0=SEQ, n_seq=DEC_BATCH, seq_len=DEC_SEQ,
                        tt=DEC_SEQ)
    s0_p = jnp.zeros((1, RW_HEADS, RW_HEAD_DIM, RW_HEAD_DIM), F32)
    rw_parts = []
    states = []
    for (r, w, k, v, kk, b, g), s0, n_seq, seq_len, tt in (
            (prep_p, s0_p, 1, SEQ, 128), (prep_s, state_rwkv[0], DEC_BATCH, DEC_SEQ, DEC_SEQ)):
        y, s_new = _rwkv_scan(r, w, k, v, kk, b, jnp.swapaxes(s0, -1, -2),
                              n_seq=n_seq, seq_len=seq_len, tt=tt)
        rw_parts.append(_rwkv_post(y, r, k, v, g, lp, 256))
        states.append(jnp.swapaxes(s_new, -1, -2))
    a_rw = jnp.concatenate(rw_parts, axis=0)

    bias_p = _band_prompt_bias(att_rel_bias[0])
    bias_s1, bias_s2 = _band_sample_bias(att_rel_bias[0])
    att_p = _band_prompt(qkv, bias_p)
    att_s = _band_sample(qkv, cache_att_k[0].reshape(DEC_BATCH, ATT_PAST, ATT_WIDTH),
                         cache_att_v[0].reshape(DEC_BATCH, ATT_PAST, ATT_WIDTH), bias_s1, bias_s2)
    a_att = jnp.concatenate([att_p, att_s], axis=0)

    mem_h = _rmsnorm(mem_prompt.reshape(N_MEM, D_MODEL), mem_norm_g[0], N_MEM, BF16)
    w_kv = w_mem_kv[0]
    mk_p = _matmul(functools.partial(_mm_headnorm_kernel, head=MEM_HEAD_DIM, n_norm_tiles=MEM_HEADS),
                   mem_h, w_kv[:, :MEM_WIDTH].astype(BF16),
                   (jnp.tile(mem_k_g[0], MEM_HEADS).reshape(1, MEM_WIDTH),),
                   (pl.BlockSpec((1, 512), lambda j, i: (0, j)),),
                   tm=N_MEM, tn=512, out_dtype=F32, name="mem_k")
    mv_p = _matmul(_mm_plain_kernel, mem_h, w_kv[:, MEM_WIDTH:].astype(BF16), (), (),
                   tm=N_MEM, tn=512, out_dtype=F32, name="mem_v")
    mem_p = _mem_attn(q_mem, mk_p.reshape(1, N_MEM, MEM_WIDTH), mv_p.reshape(1, N_MEM, MEM_WIDTH),
                      row0=0, n_rows=SEQ, tq=512, rows_per_mem=SEQ)
    mem_s = _mem_attn(q_mem, cache_mem_k[0].reshape(DEC_BATCH, N_MEM, MEM_WIDTH),
                      cache_mem_v[0].reshape(DEC_BATCH, N_MEM, MEM_WIDTH),
                      row0=SEQ, n_rows=n_samp, tq=DEC_SEQ, rows_per_mem=DEC_SEQ)
    a_mem = jnp.concatenate([mem_p, mem_s], axis=0)

    merged = _gated_merge(a_rw, a_att, a_mem, w_rwkv_out[0].astype(BF16), w_att_out[0].astype(BF16),
                          w_mem_out[0].astype(BF16), gate, 528, 512)
    x1 = _matmul(_mm_residual_kernel, merged, w_o[0].astype(BF16), (x,),
                 (pl.BlockSpec((1056, 512), lambda j, i: (i, j)),),
                 tm=1056, tn=512, out_dtype=F32, name="proj_out")

    w_router = jnp.zeros((D_MODEL, ROUTER_COLS), F32)
    w_router = w_router.at[:, :N_GROUPS].set(w_router_group[0])
    w_router = w_router.at[:, N_GROUPS:N_GROUPS + N_EXPERTS].set(w_router_expert[0])
    b_router = jnp.zeros((1, ROUTER_COLS), F32)
    b_router = b_router.at[0, :N_GROUPS].set(b_router_group[0])
    b_router = b_router.at[0, N_GROUPS:N_GROUPS + N_EXPERTS].set(b_router_expert[0])
    h2, route_id, route_w = _norm_router(x1, norm_ffn_g[0], w_router, b_router, 264)
    in_tok, pos, blk_e, blk_ok, blk_x0, blk_rows = _moe_layout(route_id[:, :TOP_K])
    yb = _expert_ffn(jnp.take(h2, in_tok, axis=0, mode="clip"), blk_e, blk_ok, blk_x0, blk_rows,
                     w_exp_gate[0], w_exp_up[0], w_exp_down[0])
    take_rows = lambda idx: jnp.take(yb, idx, axis=0, mode="clip")

    def combine(rows):
        return x1[rows] + (route_w[rows, 0:1] * take_rows(pos[rows, 0])
                           + route_w[rows, 1:2] * take_rows(pos[rows, 1]))

    y_prompt = combine(slice(0, SEQ))
    y_sample = combine(slice(SEQ, N_TOK))

    heads = lambda t, bsz, rows: t.reshape(1, bsz, rows, ATT_HEADS, ATT_HEAD_DIM)
    k_cols = slice(ATT_WIDTH, 2 * ATT_WIDTH)
    v_cols = slice(2 * ATT_WIDTH, 3 * ATT_WIDTH)
    keep = SEQ - N_PREV_CHUNKS * CHUNK
    z_s = z_rw[SEQ:, :RW_COLS].reshape(DEC_BATCH, DEC_SEQ, RW_COLS)
    return (
        y_prompt.reshape(1, SEQ, D_MODEL),
        y_sample.reshape(DEC_BATCH, DEC_SEQ, D_MODEL),
        heads(qkv[keep:SEQ, k_cols], 1, SEQ - keep),
        heads(qkv[keep:SEQ, v_cols], 1, SEQ - keep),
        states[0].reshape(1, 1, RW_HEADS, RW_HEAD_DIM, RW_HEAD_DIM),
        z_rw[SEQ - 1:SEQ, :RW_COLS].reshape(1, 1, 1, RW_COLS),
        mk_p.reshape(1, 1, N_MEM, MEM_HEADS, MEM_HEAD_DIM),
        mv_p.reshape(1, 1, N_MEM, MEM_HEADS, MEM_HEAD_DIM),
        heads(qkv[SEQ:, k_cols], DEC_BATCH, DEC_SEQ),
        heads(qkv[SEQ:, v_cols], DEC_BATCH, DEC_SEQ),
        states[1].reshape(1, DEC_BATCH, RW_HEADS, RW_HEAD_DIM, RW_HEAD_DIM),
        z_s[:, -1:, :].reshape(1, DEC_BATCH, 1, RW_COLS),
    )
```
